```python
import jax, jax.numpy as jnp
from jax import lax
import numpy as np

D_MODEL = 1024
BATCH = 8
SEQ = 8192
DEPTH = 1

N_MEM = 256
D_FF = 2816
CONV_CH = 512
CONV_K = 31
M_HEADS = 4
M_HEAD_DIM = 128
M_WIDTH = M_HEADS * M_HEAD_DIM
QK_CONV_K = 4
CHUNK = 64
MIX_WIDTH = CONV_CH + M_WIDTH
IN_COLS = 2 * CONV_CH + 4 * M_WIDTH + 2 * M_HEADS
X_HEADS = 4
X_HEAD_DIM = D_MODEL // X_HEADS
EPS = 1e-6

kernel_name = 'hybrid_conv_mlstm_parallel_heads_macaron_sandwich_memxattn'


def rmsnorm(x, g):
    xf = x.astype(jnp.float32)
    y = xf * lax.rsqrt(jnp.mean(xf * xf, axis=-1, keepdims=True) + EPS)
    return (y * g.astype(jnp.float32)).astype(x.dtype)


def layernorm(x, g, b):
    xf = x.astype(jnp.float32)
    mu = jnp.mean(xf, axis=-1, keepdims=True)
    var = jnp.mean(jnp.square(xf - mu), axis=-1, keepdims=True)
    y = (xf - mu) * lax.rsqrt(var + 1e-5)
    return (y * g.astype(jnp.float32) + b.astype(jnp.float32)).astype(x.dtype)


def swiglu(x, w_gate, w_up, w_down):
    return (jax.nn.silu(x @ w_gate) * (x @ w_up)) @ w_down


def causal_dwconv(x, w, b):
    k = w.shape[0]
    y = lax.conv_general_dilated(x, w[:, None, :], window_strides=(1,), padding=[(k - 1, 0)],
                                 dimension_numbers=('NWC', 'WIO', 'NWC'),
                                 feature_group_count=x.shape[-1])
    return y + b


def mlstm_chunkwise(q, k, v, log_i, log_f):
    bsz, seq, nh, dk = q.shape
    dv = v.shape[-1]
    nc = seq // CHUNK

    def to_chunks(a):
        return jnp.transpose(a.reshape(bsz, nc, CHUNK, nh, -1), (1, 0, 3, 2, 4))

    qc, kc, vc = to_chunks(q), to_chunks(k), to_chunks(v)
    lic = to_chunks(log_i[..., None])[..., 0]
    lfc = to_chunks(log_f[..., None])[..., 0]
    causal = jnp.tril(jnp.ones((CHUNK, CHUNK), dtype=bool))

    def step(carry, xs):
        c_state, n_state, m_state = carry
        q_, k_, v_, li, lf = xs
        b = jnp.cumsum(lf, axis=-1)
        dmat = b[..., :, None] - b[..., None, :] + li[..., None, :]
        dmat = jnp.where(causal, dmat, -jnp.inf)
        inter = b + m_state[..., None]
        m_t = jnp.maximum(inter, jnp.max(dmat, axis=-1))
        w = jnp.exp(dmat - m_t[..., None]) * jnp.einsum('bhtd,bhsd->bhts', q_, k_)
        a = jnp.exp(inter - m_t)
        num = jnp.einsum('bhts,bhse->bhte', w, v_) + a[..., None] * jnp.einsum('bhtd,bhde->bhte', q_, c_state)
        den = jnp.sum(w, axis=-1) + a * jnp.einsum('bhtd,bhd->bht', q_, n_state)
        h = num / jnp.maximum(jnp.abs(den), jnp.exp(-m_t))[..., None]
        b_end = b[..., -1]
        g = b_end[..., None] - b + li
        m_new = jnp.maximum(b_end + m_state, jnp.max(g, axis=-1))
        decay = jnp.exp(b_end + m_state - m_new)
        wg = jnp.exp(g - m_new[..., None])
        c_state = decay[..., None, None] * c_state + jnp.einsum('bhs,bhsd,bhse->bhde', wg, k_, v_)
        n_state = decay[..., None] * n_state + jnp.einsum('bhs,bhsd->bhd', wg, k_)
        return (c_state, n_state, m_new), h

    init = (jnp.zeros((bsz, nh, dk, dv), jnp.float32),
            jnp.zeros((bsz, nh, dk), jnp.float32),
            jnp.zeros((bsz, nh), jnp.float32))
    _, hs = lax.scan(step, init, (qc, kc, vc, lic, lfc))
    return jnp.transpose(hs, (1, 0, 3, 2, 4)).reshape(bsz, seq, nh, dv)


def parallel_mixer(hn, w_in, conv_w, conv_b, conv_ln_g, conv_ln_b, qk_conv_w, qk_conv_b,
                   b_igate, b_fgate, mlstm_norm_g, w_out):
    bsz, seq, _ = hn.shape
    z = hn @ w_in
    offs = np.cumsum([CONV_CH, CONV_CH, M_WIDTH, M_WIDTH, M_WIDTH, M_WIDTH, M_HEADS]).tolist()
    c_val, c_gate, q, k, v, o, ig, fg = jnp.split(z, offs, axis=-1)

    u = c_val * jax.nn.sigmoid(c_gate)
    u = causal_dwconv(u, conv_w, conv_b)
    u = jax.nn.silu(layernorm(u, conv_ln_g, conv_ln_b))

    qk = jax.nn.silu(causal_dwconv(jnp.concatenate([q, k], axis=-1), qk_conv_w, qk_conv_b))
    q, k = jnp.split(qk, 2, axis=-1)
    f32 = jnp.float32
    q = q.reshape(bsz, seq, M_HEADS, M_HEAD_DIM).astype(f32)
    k = k.reshape(bsz, seq, M_HEADS, M_HEAD_DIM).astype(f32) * (M_HEAD_DIM ** -0.5)
    v = v.reshape(bsz, seq, M_HEADS, M_HEAD_DIM).astype(f32)
    log_i = (ig + b_igate).astype(f32)
    log_f = jax.nn.log_sigmoid((fg + b_fgate).astype(f32))
    h = mlstm_chunkwise(q, k, v, log_i, log_f)
    h = h * lax.rsqrt(jnp.mean(h * h, axis=-1, keepdims=True) + EPS)
    h = h.reshape(bsz, seq, M_WIDTH) * mlstm_norm_g.astype(f32)
    h = jax.nn.sigmoid(o) * h.astype(o.dtype)

    return jnp.concatenate([u, h], axis=-1) @ w_out


def memory_cross_attention(hn, memn, wq, wk, wv, wo):
    bsz, seq, _ = hn.shape
    n_mem = memn.shape[1]
    q = (hn @ wq).reshape(bsz, seq, X_HEADS, X_HEAD_DIM)
    k = (memn @ wk).reshape(bsz, n_mem, X_HEADS, X_HEAD_DIM)
    v = (memn @ wv).reshape(bsz, n_mem, X_HEADS, X_HEAD_DIM)
    s = jnp.einsum('bshd,bmhd->bhsm', q, k).astype(jnp.float32) * (X_HEAD_DIM ** -0.5)
    p = jax.nn.softmax(s, axis=-1).astype(v.dtype)
    out = jnp.einsum('bhsm,bmhd->bshd', p, v).reshape(bsz, seq, X_HEADS * X_HEAD_DIM)
    return out @ wo


def setup_inputs(seed: int = 0) -> dict:
    key = jax.random.key(seed)
    ks = iter(jax.random.split(key, 48))

    def nrm(shape, scale):
        return jax.random.normal(next(ks), shape, jnp.float32) * scale

    def gain(shape):
        return 1.0 + 0.05 * jax.random.normal(next(ks), shape, jnp.float32)

    L = DEPTH
    d = D_MODEL
    return {
        'x': nrm((BATCH, SEQ, d), 1.0),
        'mem': nrm((BATCH, N_MEM, d), 1.0),
        'ffn1_pre_g': gain((L, d)),
        'ffn1_w_gate': nrm((L, d, D_FF), d ** -0.5),
        'ffn1_w_up': nrm((L, d, D_FF), d ** -0.5),
        'ffn1_w_down': nrm((L, D_FF, d), D_FF ** -0.5),
        'ffn1_post_g': gain((L, d)),
        'mix_pre_g': gain((L, d)),
        'w_in': nrm((L, d, IN_COLS), d ** -0.5),
        'conv_w': nrm((L, CONV_K, CONV_CH), CONV_K ** -0.5),
        'conv_b': nrm((L, CONV_CH), 0.02),
        'conv_ln_g': gain((L, CONV_CH)),
        'conv_ln_b': nrm((L, CONV_CH), 0.02),
        'qk_conv_w': nrm((L, QK_CONV_K, 2 * M_WIDTH), QK_CONV_K ** -0.5),
        'qk_conv_b': nrm((L, 2 * M_WIDTH), 0.02),
        'b_igate': nrm((L, M_HEADS), 0.1),
        'b_fgate': jnp.linspace(3.0, 6.0, M_HEADS, dtype=jnp.float32)[None, :] + nrm((L, M_HEADS), 0.1),
        'mlstm_norm_g': gain((L, M_WIDTH)),
        'w_out': nrm((L, MIX_WIDTH, d), MIX_WIDTH ** -0.5),
        'mix_post_g': gain((L, d)),
        'xattn_pre_g': gain((L, d)),
        'mem_norm_g': gain((L, d)),
        'xattn_wq': nrm((L, d, d), d ** -0.5),
        'xattn_wk': nrm((L, d, d), d ** -0.5),
        'xattn_wv': nrm((L, d, d), d ** -0.5),
        'xattn_wo': nrm((L, d, d), d ** -0.5),
        'xattn_post_g': gain((L, d)),
        'ffn2_pre_g': gain((L, d)),
        'ffn2_w_gate': nrm((L, d, D_FF), d ** -0.5),
        'ffn2_w_up': nrm((L, d, D_FF), d ** -0.5),
        'ffn2_w_down': nrm((L, D_FF, d), D_FF ** -0.5),
        'ffn2_post_g': gain((L, d)),
    }


def reference(x, mem, ffn1_pre_g, ffn1_w_gate, ffn1_w_up, ffn1_w_down, ffn1_post_g,
              mix_pre_g, w_in, conv_w, conv_b, conv_ln_g, conv_ln_b, qk_conv_w, qk_conv_b,
              b_igate, b_fgate, mlstm_norm_g, w_out, mix_post_g,
              xattn_pre_g, mem_norm_g, xattn_wq, xattn_wk, xattn_wv, xattn_wo, xattn_post_g,
              ffn2_pre_g, ffn2_w_gate, ffn2_w_up, ffn2_w_down, ffn2_post_g):
    h = x
    for l in range(DEPTH):
        f = swiglu(rmsnorm(h, ffn1_pre_g[l]), ffn1_w_gate[l], ffn1_w_up[l], ffn1_w_down[l])
        h = h + 0.5 * rmsnorm(f, ffn1_post_g[l])
        m = parallel_mixer(rmsnorm(h, mix_pre_g[l]), w_in[l], conv_w[l], conv_b[l], conv_ln_g[l],
                           conv_ln_b[l], qk_conv_w[l], qk_conv_b[l], b_igate[l], b_fgate[l],
                           mlstm_norm_g[l], w_out[l])
        h = h + rmsnorm(m, mix_post_g[l])
        c = memory_cross_attention(rmsnorm(h, xattn_pre_g[l]), rmsnorm(mem, mem_norm_g[l]),
                                   xattn_wq[l], xattn_wk[l], xattn_wv[l], xattn_wo[l])
        h = h + rmsnorm(c, xattn_post_g[l])
        f = swiglu(rmsnorm(h, ffn2_pre_g[l]), ffn2_w_gate[l], ffn2_w_up[l], ffn2_w_down[l])
        h = h + 0.5 * rmsnorm(f, ffn2_post_g[l])
    return h
```

```python
import functools

import jax
import jax.numpy as jnp
from jax import lax
from jax.experimental import pallas as pl
from jax.experimental.pallas import tpu as pltpu

F32 = jnp.float32
BF16 = jnp.bfloat16

RMS_EPS = 1e-6
LN_EPS = 1e-5

M_HEADS = 4
M_HEAD_DIM = 128
X_HEADS = 4
CONV_HALO = 32
QK_HALO = 8
MCHUNK = 128
FGATE_LANE = 8

VMEM_LIMIT_V7X = 60 * 1024 * 1024


def _rms(x, g):
    ms = jnp.mean(x * x, axis=-1, keepdims=True)
    return x * lax.rsqrt(ms + RMS_EPS) * g


def _silu(x):
    return x * jax.nn.sigmoid(x)


def _dot(a, b):
    return jnp.dot(a, b, preferred_element_type=F32)


def _resident(shape):
    return pl.BlockSpec(shape, lambda *_: (0,) * len(shape), pipeline_mode=pl.Buffered(1))


def _ffn_kernel(h_ref, pre_g_ref, wg_ref, wu_ref, wd_ref, post_g_ref, o_ref):
    h = h_ref[...]
    hn = _rms(h, pre_g_ref[...]).astype(BF16)
    gate = _dot(hn, wg_ref[...])
    up = _dot(hn, wu_ref[...])
    a = (_silu(gate) * up).astype(BF16)
    f = _dot(a, wd_ref[...])
    o_ref[...] = h + 0.5 * _rms(f, post_g_ref[...])


def _ffn(h2d, pre_g, w_gate, w_up, w_down, post_g, *, tm):
    t, d = h2d.shape
    dff = w_gate.shape[1]
    tile = pl.BlockSpec((tm, d), lambda i: (i, 0))
    return pl.pallas_call(
        _ffn_kernel,
        out_shape=jax.ShapeDtypeStruct((t, d), F32),
        grid=(t // tm,),
        in_specs=[tile, _resident((1, d)), _resident((d, dff)), _resident((d, dff)),
                  _resident((dff, d)), _resident((1, d))],
        out_specs=tile,
        compiler_params=pltpu.CompilerParams(
            dimension_semantics=("arbitrary",), vmem_limit_bytes=VMEM_LIMIT_V7X),
        name="ffn",
    )(h2d, pre_g, w_gate, w_up, w_down, post_g)


def _mixer_kernel(h_ref, pre_g_ref, w_main_ref, w_gate_ref, conv_w_ref, conv_b_ref, ln_g_ref, ln_b_ref,
                  qk_w_ref, qk_b_ref, gbias_ref, mnorm_g_ref, w_out_ref, post_g_ref, o_ref,
                  ubuf, qkbuf, cext, mstate, mixin, *, tm, conv_ch, conv_k, qk_k):
    mw = M_HEADS * M_HEAD_DIM
    hd = M_HEAD_DIM

    @pl.when(pl.program_id(1) == 0)
    def _():
        ubuf[0:CONV_HALO, :] = jnp.zeros((CONV_HALO, conv_ch), F32)
        qkbuf[0:QK_HALO, :] = jnp.zeros((QK_HALO, 2 * mw), F32)
        cext[...] = jnp.zeros(cext.shape, F32)
        mstate[...] = jnp.zeros(mstate.shape, F32)

    h = h_ref[...]
    hn = _rms(h, pre_g_ref[...]).astype(BF16)
    z = _dot(hn, w_main_ref[...])
    zg = _dot(hn, w_gate_ref[...])

    u = z[:, 0:conv_ch] * jax.nn.sigmoid(z[:, conv_ch:2 * conv_ch])
    ubuf[CONV_HALO:CONV_HALO + tm, :] = u
    base = CONV_HALO - (conv_k - 1)
    acc = jnp.broadcast_to(conv_b_ref[...], (tm, conv_ch))
    for r in range(8):
        taps = list(range(r, conv_k, 8))
        span = tm + 8 * (len(taps) - 1)
        ur = ubuf[pl.ds(base + r, span), :]
        for a, j in enumerate(taps):
            acc = acc + ur[8 * a:8 * a + tm, :] * conv_w_ref[j:j + 1, :]
    ubuf[0:CONV_HALO, :] = ubuf[tm:tm + CONV_HALO, :]
    mu = jnp.mean(acc, axis=-1, keepdims=True)
    cen = acc - mu
    var = jnp.mean(cen * cen, axis=-1, keepdims=True)
    yn = cen * lax.rsqrt(var + LN_EPS) * ln_g_ref[...] + ln_b_ref[...]
    mixin[:, 0:conv_ch] = _silu(yn).astype(BF16)

    qkbuf[QK_HALO:QK_HALO + tm, :] = z[:, 2 * conv_ch:2 * conv_ch + 2 * mw]
    qbase = QK_HALO - (qk_k - 1)
    qacc = jnp.broadcast_to(qk_b_ref[...], (tm, 2 * mw))
    for j in range(qk_k):
        qacc = qacc + qkbuf[pl.ds(qbase + j, tm), :] * qk_w_ref[j:j + 1, :]
    qkbuf[0:QK_HALO, :] = qkbuf[tm:tm + QK_HALO, :]
    qk = _silu(qacc)
    q = qk[:, 0:mw].astype(BF16)
    k = qk[:, mw:2 * mw] * (hd ** -0.5)
    v = z[:, 2 * conv_ch + 2 * mw:2 * conv_ch + 3 * mw].astype(BF16)
    og = jax.nn.sigmoid(z[:, 2 * conv_ch + 3 * mw:2 * conv_ch + 4 * mw])

    zgb = zg + gbias_ref[...]
    log_f = jnp.minimum(zgb, 0.0) - jnp.log1p(jnp.exp(-jnp.abs(zgb)))
    lane = lax.broadcasted_iota(jnp.int32, zgb.shape, 1)
    gates_t = jnp.where(lane >= FGATE_LANE, log_f, zgb).T
    li_all = gates_t[0:8, :]
    lf_all = gates_t[FGATE_LANE:FGATE_LANE + 8, :]

    row_i = lax.broadcasted_iota(jnp.int32, (MCHUNK, MCHUNK), 0)
    col_i = lax.broadcasted_iota(jnp.int32, (MCHUNK, MCHUNK), 1)
    causal = col_i <= row_i
    lane8 = lax.broadcasted_iota(jnp.int32, (8, MCHUNK), 1)
    ones_col = (lax.broadcasted_iota(jnp.int32, (MCHUNK, hd), 1) == 0).astype(BF16)

    for c in range(tm // MCHUNK):
        rows = slice(c * MCHUNK, (c + 1) * MCHUNK)
        li = li_all[:, rows]
        lf = lf_all[:, rows]
        b = lf
        for d in (1, 2, 4, 8, 16, 32, 64):
            b = b + jnp.where(lane8 >= d, pltpu.roll(b, d, axis=1), 0.0)
        b_end = jnp.sum(lf, axis=1, keepdims=True)
        m_prev = mstate[:, 0:1]
        g = b_end - b + li
        m_new = jnp.maximum(b_end + m_prev, jnp.max(g, axis=1, keepdims=True))
        decay = jnp.exp(b_end + m_prev - m_new)
        wg = jnp.exp(g - m_new)
        mstate[...] = jnp.broadcast_to(m_new, mstate.shape)

        for hh in range(M_HEADS):
            cols = slice(hh * hd, (hh + 1) * hd)
            qh = q[rows, cols]
            kt = k[rows, cols].T
            vh = v[rows, cols]
            b_col = jnp.sum(jnp.where(causal, lf[hh:hh + 1, :], 0.0), axis=1, keepdims=True)
            dmat = jnp.where(causal, b_col - b[hh:hh + 1, :] + li[hh:hh + 1, :], -jnp.inf)
            inter = b_col + m_prev[hh:hh + 1, :]
            m_t = jnp.maximum(inter, jnp.max(dmat, axis=1, keepdims=True))
            s = _dot(qh, kt.astype(BF16))
            w = jnp.exp(dmat - m_t) * s
            a = jnp.exp(inter - m_t)
            ce = cext[hh]
            qc = _dot(qh, ce.astype(BF16))
            num = _dot(w.astype(BF16), vh) + a * qc[:, 0:hd]
            den = jnp.sum(w, axis=1, keepdims=True) + a * qc[:, hd:hd + 1]
            hv = num / jnp.maximum(jnp.abs(den), jnp.exp(-m_t))
            hv = hv * lax.rsqrt(jnp.mean(hv * hv, axis=-1, keepdims=True) + RMS_EPS)
            hv = hv * mnorm_g_ref[:, cols]
            mixin[rows, conv_ch + hh * hd:conv_ch + (hh + 1) * hd] = (og[rows, cols] * hv).astype(BF16)

            ktw = (kt * wg[hh:hh + 1, :]).astype(BF16)
            vext = jnp.concatenate([vh, ones_col], axis=1)
            cext[hh] = decay[hh:hh + 1, :] * ce + _dot(ktw, vext)

    m = _dot(mixin[...], w_out_ref[...])
    o_ref[...] = h + _rms(m, post_g_ref[...])


def _mixer(h3d, pre_g, w_main, w_gate, conv_w, conv_b, ln_g, ln_b, qk_w, qk_b, gbias, mnorm_g, w_out, post_g,
           *, tm, conv_k, qk_k):
    bsz, seq, d = h3d.shape
    conv_ch = conv_w.shape[1]
    mw = M_HEADS * M_HEAD_DIM
    assert tm % MCHUNK == 0 and seq % tm == 0 and tm >= CONV_HALO
    assert conv_k - 1 <= CONV_HALO and qk_k - 1 <= QK_HALO
    tile = pl.BlockSpec((None, tm, d), lambda b, j: (b, j, 0))
    consts = [pre_g, w_main, w_gate, conv_w, conv_b, ln_g, ln_b, qk_w, qk_b, gbias, mnorm_g, w_out, post_g]
    kern = functools.partial(_mixer_kernel, tm=tm, conv_ch=conv_ch, conv_k=conv_k, qk_k=qk_k)
    return pl.pallas_call(
        kern,
        out_shape=jax.ShapeDtypeStruct((bsz, seq, d), F32),
        grid=(bsz, seq // tm),
        in_specs=[tile] + [_resident(c.shape) for c in consts],
        out_specs=tile,
        scratch_shapes=[
            pltpu.VMEM((CONV_HALO + tm, conv_ch), F32),
            pltpu.VMEM((QK_HALO + tm, 2 * mw), F32),
            pltpu.VMEM((M_HEADS, M_HEAD_DIM, 2 * M_HEAD_DIM), F32),
            pltpu.VMEM((8, MCHUNK), F32),
            pltpu.VMEM((tm, conv_ch + mw), BF16),
        ],
        compiler_params=pltpu.CompilerParams(
            dimension_semantics=("arbitrary", "arbitrary"), vmem_limit_bytes=VMEM_LIMIT_V7X),
        name="mixer",
    )(h3d, *consts)


def _memkv_kernel(mem_ref, g_ref, wk_ref, wv_ref, kt_ref, v_ref, *, scale):
    memn = _rms(mem_ref[...], g_ref[...]).astype(BF16)
    k = _dot(memn, wk_ref[...]) * scale
    kt_ref[...] = k.T.astype(BF16)
    v_ref[...] = _dot(memn, wv_ref[...]).astype(BF16)


def _memkv(mem, g, wk, wv, *, scale):
    bsz, n_mem, d = mem.shape
    return pl.pallas_call(
        functools.partial(_memkv_kernel, scale=scale),
        out_shape=(jax.ShapeDtypeStruct((bsz, d, n_mem), BF16), jax.ShapeDtypeStruct((bsz, n_mem, d), BF16)),
        grid=(bsz,),
        in_specs=[pl.BlockSpec((None, n_mem, d), lambda b: (b, 0, 0)), _resident((1, d)),
                  _resident((d, d)), _resident((d, d))],
        out_specs=(pl.BlockSpec((None, d, n_mem), lambda b: (b, 0, 0)),
                   pl.BlockSpec((None, n_mem, d), lambda b: (b, 0, 0))),
        compiler_params=pltpu.CompilerParams(
            dimension_semantics=("arbitrary",), vmem_limit_bytes=VMEM_LIMIT_V7X),
        name="memkv",
    )(mem, g, wk, wv)


def _xattn_kernel(h_ref, pre_g_ref, wq_ref, kt_ref, v_ref, wo_ref, post_g_ref, o_ref):
    h = h_ref[...]
    d = h.shape[-1]
    xd = d // X_HEADS
    hn = _rms(h, pre_g_ref[...]).astype(BF16)
    q = _dot(hn, wq_ref[...]).astype(BF16)
    outs = []
    for hh in range(X_HEADS):
        cols = slice(hh * xd, (hh + 1) * xd)
        s = _dot(q[:, cols], kt_ref[cols, :])
        p = jnp.exp(s - jnp.max(s, axis=-1, keepdims=True))
        p = p / jnp.sum(p, axis=-1, keepdims=True)
        outs.append(_dot(p.astype(BF16), v_ref[:, cols]))
    att = jnp.concatenate(outs, axis=-1).astype(BF16)
    c = _dot(att, wo_ref[...])
    o_ref[...] = h + _rms(c, post_g_ref[...])


def _xattn(h3d, pre_g, wq, kt, v, wo, post_g, *, tm):
    bsz, seq, d = h3d.shape
    n_mem = v.shape[1]
    tile = pl.BlockSpec((None, tm, d), lambda b, j: (b, j, 0))
    return pl.pallas_call(
        _xattn_kernel,
        out_shape=jax.ShapeDtypeStruct((bsz, seq, d), F32),
        grid=(bsz, seq // tm),
        in_specs=[tile, _resident((1, d)), _resident((d, d)),
                  pl.BlockSpec((None, d, n_mem), lambda b, j: (b, 0, 0)),
                  pl.BlockSpec((None, n_mem, d), lambda b, j: (b, 0, 0)),
                  _resident((d, d)), _resident((1, d))],
        out_specs=tile,
        compiler_params=pltpu.CompilerParams(
            dimension_semantics=("arbitrary", "arbitrary"), vmem_limit_bytes=VMEM_LIMIT_V7X),
        name="xattn",
    )(h3d, pre_g, wq, kt, v, wo, post_g)


def _row(x):
    return x.reshape(1, -1).astype(F32)


def _pad_rows(x, n):
    return jnp.pad(x.astype(F32), ((0, n - x.shape[0]), (0, 0)))


def kernel(x, mem, ffn1_pre_g, ffn1_w_gate, ffn1_w_up, ffn1_w_down, ffn1_post_g, mix_pre_g, w_in, conv_w, conv_b, conv_ln_g, conv_ln_b, qk_conv_w, qk_conv_b, b_igate, b_fgate, mlstm_norm_g, w_out, mix_post_g, xattn_pre_g, mem_norm_g, xattn_wq, xattn_wk, xattn_wv, xattn_wo, xattn_post_g, ffn2_pre_g, ffn2_w_gate, ffn2_w_up, ffn2_w_down, ffn2_post_g):
    bsz, seq, d = x.shape
    depth = w_in.shape[0]
    conv_k, conv_ch = conv_w.shape[1], conv_w.shape[2]
    qk_k = qk_conv_w.shape[1]
    mw = M_HEADS * M_HEAD_DIM
    main_cols = 2 * conv_ch + 4 * mw
    assert w_in.shape[2] == main_cols + 2 * M_HEADS

    h = x
    for l in range(depth):
        h = _ffn(h.reshape(bsz * seq, d), _row(ffn1_pre_g[l]), ffn1_w_gate[l].astype(BF16),
                 ffn1_w_up[l].astype(BF16), ffn1_w_down[l].astype(BF16), _row(ffn1_post_g[l]),
                 tm=512).reshape(bsz, seq, d)

        w_gate = jnp.zeros((d, 128), F32)
        w_gate = w_gate.at[:, 0:M_HEADS].set(w_in[l][:, main_cols:main_cols + M_HEADS])
        w_gate = w_gate.at[:, FGATE_LANE:FGATE_LANE + M_HEADS].set(w_in[l][:, main_cols + M_HEADS:])
        gbias = jnp.zeros((1, 128), F32)
        gbias = gbias.at[0, 0:M_HEADS].set(b_igate[l]).at[0, FGATE_LANE:FGATE_LANE + M_HEADS].set(b_fgate[l])
        h = _mixer(h, _row(mix_pre_g[l]), w_in[l][:, :main_cols].astype(BF16), w_gate.astype(BF16),
                   _pad_rows(conv_w[l], 32), _row(conv_b[l]), _row(conv_ln_g[l]), _row(conv_ln_b[l]),
                   _pad_rows(qk_conv_w[l], 8), _row(qk_conv_b[l]), gbias, _row(mlstm_norm_g[l]),
                   w_out[l].astype(BF16), _row(mix_post_g[l]), tm=256, conv_k=conv_k, qk_k=qk_k)

        kt, v = _memkv(mem, _row(mem_norm_g[l]), xattn_wk[l].astype(BF16), xattn_wv[l].astype(BF16),
                       scale=(d // X_HEADS) ** -0.5)
        h = _xattn(h, _row(xattn_pre_g[l]), xattn_wq[l].astype(BF16), kt, v, xattn_wo[l].astype(BF16),
                   _row(xattn_post_g[l]), tm=512)

        h = _ffn(h.reshape(bsz * seq, d), _row(ffn2_pre_g[l]), ffn2_w_gate[l].astype(BF16),
                 ffn2_w_up[l].astype(BF16), ffn2_w_down[l].astype(BF16), _row(ffn2_post_g[l]),
                 tm=512).reshape(bsz, seq, d)
    return h
```

```python
import functools

import jax
import jax.numpy as jnp
from jax import lax
from jax.experimental import pallas as pl
from jax.experimental.pallas import tpu as pltpu

F32 = jnp.float32
BF16 = jnp.bfloat16

RMS_EPS = 1e-6
LN_EPS = 1e-5

LANES = 128
M_HEADS = 4
M_HEAD_DIM = 128
X_HEADS = 4
CONV_HALO = 32
QK_HALO = 8
MCHUNK = 128
FGATE_LANE = 8

VMEM_LIMIT_V7X = 60 * 1024 * 1024


def _rms(x, g):
    ms = jnp.mean(x * x, axis=-1, keepdims=True)
    return x * lax.rsqrt(ms + RMS_EPS) * g


def _silu(x):
    return x * jax.nn.sigmoid(x)


def _dot(a, b):
    return jnp.dot(a, b, preferred_element_type=F32)


def _resident(shape):
    return pl.BlockSpec(shape, lambda *_: (0,) * len(shape), pipeline_mode=pl.Buffered(1))


def _ffn_kernel(h_ref, pre_g_ref, wg_ref, wu_ref, wd_ref, post_g_ref, o_ref):
    h = h_ref[...]
    hn = _rms(h, pre_g_ref[...]).astype(BF16)
    gate = _dot(hn, wg_ref[...])
    up = _dot(hn, wu_ref[...])
    a = (_silu(gate) * up).astype(BF16)
    f = _dot(a, wd_ref[...])
    o_ref[...] = h + 0.5 * _rms(f, post_g_ref[...])


def _ffn(h2d, pre_g, w_gate, w_up, w_down, post_g, *, tm):
    t, d = h2d.shape
    dff = w_gate.shape[1]
    tile = pl.BlockSpec((tm, d), lambda i: (i, 0))
    return pl.pallas_call(
        _ffn_kernel,
        out_shape=jax.ShapeDtypeStruct((t, d), F32),
        grid=(t // tm,),
        in_specs=[tile, _resident((1, d)), _resident((d, dff)), _resident((d, dff)),
                  _resident((dff, d)), _resident((1, d))],
        out_specs=tile,
        compiler_params=pltpu.CompilerParams(
            dimension_semantics=("arbitrary",), vmem_limit_bytes=VMEM_LIMIT_V7X),
        name="ffn",
    )(h2d, pre_g, w_gate, w_up, w_down, post_g)


def _causal_dwconv(buf, x, w_ref, b_ref, k, halo):
    tm, ch = x.shape
    base = halo - (k - 1)
    cols = []
    for cb in range(ch // LANES):
        lanes = slice(cb * LANES, (cb + 1) * LANES)
        buf[cb, halo:halo + tm, :] = x[:, lanes]
        acc = jnp.broadcast_to(b_ref[:, lanes], (tm, LANES))
        for j in range(k):
            acc = acc + buf[cb, pl.ds(base + j, tm), :] * w_ref[j:j + 1, lanes]
        buf[cb, 0:halo, :] = buf[cb, tm:tm + halo, :]
        cols.append(acc)
    return jnp.concatenate(cols, axis=1)


def _mixer_kernel(h_ref, pre_g_ref, w_main_ref, w_gate_ref, conv_w_ref, conv_b_ref, ln_g_ref, ln_b_ref,
                  qk_w_ref, qk_b_ref, gbias_ref, mnorm_g_ref, w_out_ref, post_g_ref, o_ref,
                  ubuf, qkbuf, cext, mstate, mixin, *, tm, conv_ch, conv_k, qk_k):
    mw = M_HEADS * M_HEAD_DIM
    hd = M_HEAD_DIM

    @pl.when(pl.program_id(1) == 0)
    def _():
        ubuf[:, 0:CONV_HALO, :] = jnp.zeros((conv_ch // LANES, CONV_HALO, LANES), F32)
        qkbuf[:, 0:QK_HALO, :] = jnp.zeros((2 * mw // LANES, QK_HALO, LANES), F32)
        cext[...] = jnp.zeros(cext.shape, F32)
        mstate[...] = jnp.zeros(mstate.shape, F32)

    h = h_ref[...]
    hn = _rms(h, pre_g_ref[...]).astype(BF16)
    z = _dot(hn, w_main_ref[...])
    zg = _dot(hn, w_gate_ref[...])

    u = z[:, 0:conv_ch] * jax.nn.sigmoid(z[:, conv_ch:2 * conv_ch])
    acc = _causal_dwconv(ubuf, u, conv_w_ref, conv_b_ref, conv_k, CONV_HALO)
    mu = jnp.mean(acc, axis=-1, keepdims=True)
    cen = acc - mu
    var = jnp.mean(cen * cen, axis=-1, keepdims=True)
    yn = cen * lax.rsqrt(var + LN_EPS) * ln_g_ref[...] + ln_b_ref[...]
    mixin[:, 0:conv_ch] = _silu(yn).astype(BF16)

    qk = _silu(_causal_dwconv(qkbuf, z[:, 2 * conv_ch:2 * conv_ch + 2 * mw], qk_w_ref, qk_b_ref, qk_k, QK_HALO))
    q = qk[:, 0:mw].astype(BF16)
    k = qk[:, mw:2 * mw] * (hd ** -0.5)
    v = z[:, 2 * conv_ch + 2 * mw:2 * conv_ch + 3 * mw].astype(BF16)
    og = jax.nn.sigmoid(z[:, 2 * conv_ch + 3 * mw:2 * conv_ch + 4 * mw])

    zgb = zg + gbias_ref[...]
    log_f = jnp.minimum(zgb, 0.0) - jnp.log1p(jnp.exp(-jnp.abs(zgb)))
    lane = lax.broadcasted_iota(jnp.int32, zgb.shape, 1)
    gates_t = jnp.where(lane >= FGATE_LANE, log_f, zgb).T
    li_all = gates_t[0:8, :]
    lf_all = gates_t[FGATE_LANE:FGATE_LANE + 8, :]

    row_i = lax.broadcasted_iota(jnp.int32, (MCHUNK, MCHUNK), 0)
    col_i = lax.broadcasted_iota(jnp.int32, (MCHUNK, MCHUNK), 1)
    causal = col_i <= row_i
    lane8 = lax.broadcasted_iota(jnp.int32, (8, MCHUNK), 1)
    ones_col = (lax.broadcasted_iota(jnp.int32, (MCHUNK, hd), 1) == 0).astype(BF16)

    for c in range(tm // MCHUNK):
        rows = slice(c * MCHUNK, (c + 1) * MCHUNK)
        li = li_all[:, rows]
        lf = lf_all[:, rows]
        b = lf
        for d in (1, 2, 4, 8, 16, 32, 64):
            b = b + jnp.where(lane8 >= d, pltpu.roll(b, d, axis=1), 0.0)
        b_end = jnp.sum(lf, axis=1, keepdims=True)
        m_prev = mstate[:, 0:1]
        g = b_end - b + li
        m_new = jnp.maximum(b_end + m_prev, jnp.max(g, axis=1, keepdims=True))
        decay = jnp.exp(b_end + m_prev - m_new)
        wg = jnp.exp(g - m_new)
        mstate[...] = jnp.broadcast_to(m_new, mstate.shape)

        for hh in range(M_HEADS):
            cols = slice(hh * hd, (hh + 1) * hd)
            qh = q[rows, cols]
            kt = k[rows, cols].T
            vh = v[rows, cols]
            b_col = jnp.sum(jnp.where(causal, lf[hh:hh + 1, :], 0.0), axis=1, keepdims=True)
            dmat = jnp.where(causal, b_col - b[hh:hh + 1, :] + li[hh:hh + 1, :], -jnp.inf)
            inter = b_col + m_prev[hh:hh + 1, :]
            m_t = jnp.maximum(inter, jnp.max(dmat, axis=1, keepdims=True))
            s = _dot(qh, kt.astype(BF16))
            w = jnp.exp(dmat - m_t) * s
            a = jnp.exp(inter - m_t)
            ce = cext[hh]
            qc = _dot(qh, ce.astype(BF16))
            num = _dot(w.astype(BF16), vh) + a * qc[:, 0:hd]
            den = jnp.sum(w, axis=1, keepdims=True) + a * qc[:, hd:hd + 1]
            hv = num / jnp.maximum(jnp.abs(den), jnp.exp(-m_t))
            hv = hv * lax.rsqrt(jnp.mean(hv * hv, axis=-1, keepdims=True) + RMS_EPS)
            hv = hv * mnorm_g_ref[:, cols]
            mixin[rows, conv_ch + hh * hd:conv_ch + (hh + 1) * hd] = (og[rows, cols] * hv).astype(BF16)

            ktw = (kt * wg[hh:hh + 1, :]).astype(BF16)
            vext = jnp.concatenate([vh, ones_col], axis=1)
            cext[hh] = decay[hh:hh + 1, :] * ce + _dot(ktw, vext)

    m = _dot(mixin[...], w_out_ref[...])
    o_ref[...] = h + _rms(m, post_g_ref[...])


def _mixer(h3d, pre_g, w_main, w_gate, conv_w, conv_b, ln_g, ln_b, qk_w, qk_b, gbias, mnorm_g, w_out, post_g,
           *, tm, conv_k, qk_k):
    bsz, seq, d = h3d.shape
    conv_ch = conv_w.shape[1]
    mw = M_HEADS * M_HEAD_DIM
    assert tm % MCHUNK == 0 and seq % tm == 0 and tm >= CONV_HALO
    assert conv_k - 1 <= CONV_HALO and qk_k - 1 <= QK_HALO
    tile = pl.BlockSpec((None, tm, d), lambda b, j: (b, j, 0))
    consts = [pre_g, w_main, w_gate, conv_w, conv_b, ln_g, ln_b, qk_w, qk_b, gbias, mnorm_g, w_out, post_g]
    kern = functools.partial(_mixer_kernel, tm=tm, conv_ch=conv_ch, conv_k=conv_k, qk_k=qk_k)
    return pl.pallas_call(
        kern,
        out_shape=jax.ShapeDtypeStruct((bsz, seq, d), F32),
        grid=(bsz, seq // tm),
        in_specs=[tile] + [_resident(c.shape) for c in consts],
        out_specs=tile,
        scratch_shapes=[
            pltpu.VMEM((conv_ch // LANES, CONV_HALO + tm, LANES), F32),
            pltpu.VMEM((2 * mw // LANES, QK_HALO + tm, LANES), F32),
            pltpu.VMEM((M_HEADS, M_HEAD_DIM, 2 * M_HEAD_DIM), F32),
            pltpu.VMEM((8, MCHUNK), F32),
            pltpu.VMEM((tm, conv_ch + mw), BF16),
        ],
        compiler_params=pltpu.CompilerParams(
            dimension_semantics=("arbitrary", "arbitrary"), vmem_limit_bytes=VMEM_LIMIT_V7X),
        name="mixer",
    )(h3d, *consts)


def _memkv_kernel(mem_ref, g_ref, wk_ref, wv_ref, kt_ref, v_ref, *, scale):
    memn = _rms(mem_ref[...], g_ref[...]).astype(BF16)
    k = _dot(memn, wk_ref[...]) * scale
    kt_ref[...] = k.T.astype(BF16)
    v_ref[...] = _dot(memn, wv_ref[...]).astype(BF16)


def _memkv(mem, g, wk, wv, *, scale):
    bsz, n_mem, d = mem.shape
    return pl.pallas_call(
        functools.partial(_memkv_kernel, scale=scale),
        out_shape=(jax.ShapeDtypeStruct((bsz, d, n_mem), BF16), jax.ShapeDtypeStruct((bsz, n_mem, d), BF16)),
        grid=(bsz,),
        in_specs=[pl.BlockSpec((None, n_mem, d), lambda b: (b, 0, 0)), _resident((1, d)),
                  _resident((d, d)), _resident((d, d))],
        out_specs=(pl.BlockSpec((None, d, n_mem), lambda b: (b, 0, 0)),
                   pl.BlockSpec((None, n_mem, d), lambda b: (b, 0, 0))),
        compiler_params=pltpu.CompilerParams(
            dimension_semantics=("arbitrary",), vmem_limit_bytes=VMEM_LIMIT_V7X),
        name="memkv",
    )(mem, g, wk, wv)


def _xattn_kernel(h_ref, pre_g_ref, wq_ref, kt_ref, v_ref, wo_ref, post_g_ref, o_ref):
    h = h_ref[...]
    d = h.shape[-1]
    xd = d // X_HEADS
    hn = _rms(h, pre_g_ref[...]).astype(BF16)
    q = _dot(hn, wq_ref[...]).astype(BF16)
    outs = []
    for hh in range(X_HEADS):
        cols = slice(hh * xd, (hh + 1) * xd)
        s = _dot(q[:, cols], kt_ref[cols, :])
        p = jnp.exp(s - jnp.max(s, axis=-1, keepdims=True))
        p = p / jnp.sum(p, axis=-1, keepdims=True)
        outs.append(_dot(p.astype(BF16), v_ref[:, cols]))
    att = jnp.concatenate(outs, axis=-1).astype(BF16)
    c = _dot(att, wo_ref[...])
    o_ref[...] = h + _rms(c, post_g_ref[...])


def _xattn(h3d, pre_g, wq, kt, v, wo, post_g, *, tm):
    bsz, seq, d = h3d.shape
    n_mem = v.shape[1]
    tile = pl.BlockSpec((None, tm, d), lambda b, j: (b, j, 0))
    return pl.pallas_call(
        _xattn_kernel,
        out_shape=jax.ShapeDtypeStruct((bsz, seq, d), F32),
        grid=(bsz, seq // tm),
        in_specs=[tile, _resident((1, d)), _resident((d, d)),
                  pl.BlockSpec((None, d, n_mem), lambda b, j: (b, 0, 0)),
                  pl.BlockSpec((None, n_mem, d), lambda b, j: (b, 0, 0)),
                  _resident((d, d)), _resident((1, d))],
        out_specs=tile,
        compiler_params=pltpu.CompilerParams(
            dimension_semantics=("arbitrary", "arbitrary"), vmem_limit_bytes=VMEM_LIMIT_V7X),
        name="xattn",
    )(h3d, pre_g, wq, kt, v, wo, post_g)


def _row(x):
    return x.reshape(1, -1).astype(F32)


def _pad_rows(x, n):
    return jnp.pad(x.astype(F32), ((0, n - x.shape[0]), (0, 0)))


def kernel(x, mem, ffn1_pre_g, ffn1_w_gate, ffn1_w_up, ffn1_w_down, ffn1_post_g, mix_pre_g, w_in, conv_w, conv_b, conv_ln_g, conv_ln_b, qk_conv_w, qk_conv_b, b_igate, b_fgate, mlstm_norm_g, w_out, mix_post_g, xattn_pre_g, mem_norm_g, xattn_wq, xattn_wk, xattn_wv, xattn_wo, xattn_post_g, ffn2_pre_g, ffn2_w_gate, ffn2_w_up, ffn2_w_down, ffn2_post_g):
    bsz, seq, d = x.shape
    depth = w_in.shape[0]
    conv_k, conv_ch = conv_w.shape[1], conv_w.shape[2]
    qk_k = qk_conv_w.shape[1]
    mw = M_HEADS * M_HEAD_DIM
    main_cols = 2 * conv_ch + 4 * mw
    assert w_in.shape[2] == main_cols + 2 * M_HEADS

    h = x
    for l in range(depth):
        h = _ffn(h.reshape(bsz * seq, d), _row(ffn1_pre_g[l]), ffn1_w_gate[l].astype(BF16),
                 ffn1_w_up[l].astype(BF16), ffn1_w_down[l].astype(BF16), _row(ffn1_post_g[l]),
                 tm=512).reshape(bsz, seq, d)

        w_gate = jnp.zeros((d, 128), F32)
        w_gate = w_gate.at[:, 0:M_HEADS].set(w_in[l][:, main_cols:main_cols + M_HEADS])
        w_gate = w_gate.at[:, FGATE_LANE:FGATE_LANE + M_HEADS].set(w_in[l][:, main_cols + M_HEADS:])
        gbias = jnp.zeros((1, 128), F32)
        gbias = gbias.at[0, 0:M_HEADS].set(b_igate[l]).at[0, FGATE_LANE:FGATE_LANE + M_HEADS].set(b_fgate[l])
        h = _mixer(h, _row(mix_pre_g[l]), w_in[l][:, :main_cols].astype(BF16), w_gate.astype(BF16),
                   _pad_rows(conv_w[l], 32), _row(conv_b[l]), _row(conv_ln_g[l]), _row(conv_ln_b[l]),
                   _pad_rows(qk_conv_w[l], 8), _row(qk_conv_b[l]), gbias, _row(mlstm_norm_g[l]),
                   w_out[l].astype(BF16), _row(mix_post_g[l]), tm=512, conv_k=conv_k, qk_k=qk_k)

        kt, v = _memkv(mem, _row(mem_norm_g[l]), xattn_wk[l].astype(BF16), xattn_wv[l].astype(BF16),
                       scale=(d // X_HEADS) ** -0.5)
        h = _xattn(h, _row(xattn_pre_g[l]), xattn_wq[l].astype(BF16), kt, v, xattn_wo[l].astype(BF16),
                   _row(xattn_post_g[l]), tm=512)

        h = _ffn(h.reshape(bsz * seq, d), _row(ffn2_pre_g[l]), ffn2_w_gate[l].astype(BF16),
                 ffn2_w_up[l].astype(BF16), ffn2_w_down[l].astype(BF16), _row(ffn2_post_g[l]),
                 tm=512).reshape(bsz, seq, d)
    return h
```

```python
import functools

import jax
import jax.numpy as jnp
from jax import lax
from jax.experimental import pallas as pl
from jax.experimental.pallas import tpu as pltpu

F32 = jnp.float32
BF16 = jnp.bfloat16

RMS_EPS = 1e-6
LN_EPS = 1e-5

LANES = 128
M_HEADS = 4
M_HEAD_DIM = 128
X_HEADS = 4
CONV_HALO = 32
QK_HALO = 8
CONV_ROWS = 64
MCHUNK = 128
FGATE_LANE = 8

VMEM_LIMIT_V7X = 60 * 1024 * 1024


def _rms(x, g):
    ms = jnp.mean(x * x, axis=-1, keepdims=True)
    return x * lax.rsqrt(ms + RMS_EPS) * g


def _silu(x):
    return x * jax.nn.sigmoid(x)


def _dot(a, b):
    return jnp.dot(a, b, preferred_element_type=F32)


def _resident(shape):
    return pl.BlockSpec(shape, lambda *_: (0,) * len(shape), pipeline_mode=pl.Buffered(1))


def _ffn_kernel(h_ref, pre_g_ref, wg_ref, wu_ref, wd_ref, post_g_ref, o_ref):
    h = h_ref[...]
    hn = _rms(h, pre_g_ref[...]).astype(BF16)
    gate = _dot(hn, wg_ref[...])
    up = _dot(hn, wu_ref[...])
    a = (_silu(gate) * up).astype(BF16)
    f = _dot(a, wd_ref[...])
    o_ref[...] = h + 0.5 * _rms(f, post_g_ref[...])


def _ffn(h2d, pre_g, w_gate, w_up, w_down, post_g, *, tm):
    t, d = h2d.shape
    dff = w_gate.shape[1]
    tile = pl.BlockSpec((tm, d), lambda i: (i, 0))
    return pl.pallas_call(
        _ffn_kernel,
        out_shape=jax.ShapeDtypeStruct((t, d), F32),
        grid=(t // tm,),
        in_specs=[tile, _resident((1, d)), _resident((d, dff)), _resident((d, dff)),
                  _resident((dff, d)), _resident((1, d))],
        out_specs=tile,
        compiler_params=pltpu.CompilerParams(
            dimension_semantics=("arbitrary",), vmem_limit_bytes=VMEM_LIMIT_V7X),
        name="ffn",
    )(h2d, pre_g, w_gate, w_up, w_down, post_g)


def _dwconv_block(buf, cb, w_ref, b_ref, k, halo, tm):
    base = halo - (k - 1)
    taps = [w_ref[cb, j:j + 1, :] for j in range(k)]
    blocks = []
    for r0 in range(0, tm, CONV_ROWS):
        acc = jnp.broadcast_to(b_ref[cb], (CONV_ROWS, LANES))
        for j in range(k):
            acc = acc + buf[cb, pl.ds(base + j + r0, CONV_ROWS), :] * taps[j]
        blocks.append(acc)
    buf[cb, 0:halo, :] = buf[cb, tm:tm + halo, :]
    return jnp.concatenate(blocks, axis=0)


def _causal_dwconv(buf, x, w_ref, b_ref, k, halo):
    tm, ch = x.shape
    cols = []
    for cb in range(ch // LANES):
        buf[cb, halo:halo + tm, :] = x[:, cb * LANES:(cb + 1) * LANES]
        cols.append(_dwconv_block(buf, cb, w_ref, b_ref, k, halo, tm))
    return jnp.concatenate(cols, axis=1)


def _mixer_kernel(h_ref, pre_g_ref, w_glu_ref, w_rest_ref, w_gate_ref, conv_w_ref, conv_b_ref, ln_g_ref, ln_b_ref,
                  qk_w_ref, qk_b_ref, gbias_ref, mnorm_g_ref, w_out_ref, post_g_ref, o_ref,
                  ubuf, qkbuf, cext, mstate, mixin, hn_scr, zrest, convout, *, tm, conv_ch, conv_k, qk_k):
    mw = M_HEADS * M_HEAD_DIM
    hd = M_HEAD_DIM

    @pl.when(pl.program_id(1) == 0)
    def _():
        ubuf[:, 0:CONV_HALO, :] = jnp.zeros((conv_ch // LANES, CONV_HALO, LANES), F32)
        qkbuf[:, 0:QK_HALO, :] = jnp.zeros((2 * mw // LANES, QK_HALO, LANES), F32)
        cext[...] = jnp.zeros(cext.shape, F32)
        mstate[...] = jnp.zeros(mstate.shape, F32)

    h = h_ref[...]
    hn_scr[...] = _rms(h, pre_g_ref[...]).astype(BF16)
    za = _dot(hn_scr[...], w_glu_ref[...])
    zg = _dot(hn_scr[...], w_gate_ref[...])

    u = za[:, 0:conv_ch] * jax.nn.sigmoid(za[:, conv_ch:2 * conv_ch])
    n_cb = conv_ch // LANES
    for cb in range(n_cb):
        ubuf[cb, CONV_HALO:CONV_HALO + tm, :] = u[:, cb * LANES:(cb + 1) * LANES]

    def conv_and_project(cb, carry):
        zrest[cb] = _dot(hn_scr[...], w_rest_ref[cb])
        convout[cb] = _dwconv_block(ubuf, cb, conv_w_ref, conv_b_ref, conv_k, CONV_HALO, tm)
        return carry

    lax.fori_loop(0, n_cb, conv_and_project, 0)
    acc = jnp.concatenate([convout[cb] for cb in range(n_cb)], axis=1)
    z = jnp.concatenate([zrest[p] for p in range(n_cb)], axis=1)
    mu = jnp.mean(acc, axis=-1, keepdims=True)
    cen = acc - mu
    var = jnp.mean(cen * cen, axis=-1, keepdims=True)
    yn = cen * lax.rsqrt(var + LN_EPS) * ln_g_ref[...] + ln_b_ref[...]
    mixin[:, 0:conv_ch] = _silu(yn).astype(BF16)

    qk = _silu(_causal_dwconv(qkbuf, z[:, 0:2 * mw], qk_w_ref, qk_b_ref, qk_k, QK_HALO))
    q = qk[:, 0:mw].astype(BF16)
    k = qk[:, mw:2 * mw] * (hd ** -0.5)
    v = z[:, 2 * mw:3 * mw].astype(BF16)
    og = jax.nn.sigmoid(z[:, 3 * mw:4 * mw])

    zgb = zg + gbias_ref[...]
    log_f = jnp.minimum(zgb, 0.0) - jnp.log1p(jnp.exp(-jnp.abs(zgb)))
    lane = lax.broadcasted_iota(jnp.int32, zgb.shape, 1)
    gates_t = jnp.where(lane >= FGATE_LANE, log_f, zgb).T
    li_all = gates_t[0:8, :]
    lf_all = gates_t[FGATE_LANE:FGATE_LANE + 8, :]

    row_i = lax.broadcasted_iota(jnp.int32, (MCHUNK, MCHUNK), 0)
    col_i = lax.broadcasted_iota(jnp.int32, (MCHUNK, MCHUNK), 1)
    causal = col_i <= row_i
    lane8 = lax.broadcasted_iota(jnp.int32, (8, MCHUNK), 1)
    ones_col = (lax.broadcasted_iota(jnp.int32, (MCHUNK, hd), 1) == 0).astype(BF16)

    for c in range(tm // MCHUNK):
        rows = slice(c * MCHUNK, (c + 1) * MCHUNK)
        li = li_all[:, rows]
        lf = lf_all[:, rows]
        b = lf
        for d in (1, 2, 4, 8, 16, 32, 64):
            b = b + jnp.where(lane8 >= d, pltpu.roll(b, d, axis=1), 0.0)
        b_end = jnp.sum(lf, axis=1, keepdims=True)
        m_prev = mstate[:, 0:1]
        g = b_end - b + li
        m_new = jnp.maximum(b_end + m_prev, jnp.max(g, axis=1, keepdims=True))
        decay = jnp.exp(b_end + m_prev - m_new)
        wg = jnp.exp(g - m_new)
        mstate[...] = jnp.broadcast_to(m_new, mstate.shape)

        for hh in range(M_HEADS):
            cols = slice(hh * hd, (hh + 1) * hd)
            qh = q[rows, cols]
            kt = k[rows, cols].T
            vh = v[rows, cols]
            b_col = jnp.sum(jnp.where(causal, lf[hh:hh + 1, :], 0.0), axis=1, keepdims=True)
            dmat = jnp.where(causal, b_col - b[hh:hh + 1, :] + li[hh:hh + 1, :], -jnp.inf)
            inter = b_col + m_prev[hh:hh + 1, :]
            m_t = jnp.maximum(inter, jnp.max(dmat, axis=1, keepdims=True))
            s = _dot(qh, kt.astype(BF16))
            w = jnp.exp(dmat - m_t) * s
            a = jnp.exp(inter - m_t)
            ce = cext[hh]
            qc = _dot(qh, ce.astype(BF16))
            num = _dot(w.astype(BF16), vh) + a * qc[:, 0:hd]
            den = jnp.sum(w, axis=1, keepdims=True) + a * qc[:, hd:hd + 1]
            hv = num / jnp.maximum(jnp.abs(den), jnp.exp(-m_t))
            hv = hv * lax.rsqrt(jnp.mean(hv * hv, axis=-1, keepdims=True) + RMS_EPS)
            hv = hv * mnorm_g_ref[:, cols]
            mixin[rows, conv_ch + hh * hd:conv_ch + (hh + 1) * hd] = (og[rows, cols] * hv).astype(BF16)

            ktw = (kt * wg[hh:hh + 1, :]).astype(BF16)
            vext = jnp.concatenate([vh, ones_col], axis=1)
            cext[hh] = decay[hh:hh + 1, :] * ce + _dot(ktw, vext)

    m = _dot(mixin[...], w_out_ref[...])
    o_ref[...] = h + _rms(m, post_g_ref[...])


def _lane_blocks(x):
    rows, ch = x.shape
    return x.astype(F32).reshape(rows, ch // LANES, LANES).transpose(1, 0, 2)


def _mixer(h3d, pre_g, w_glu, w_rest, w_gate, conv_w, conv_b, ln_g, ln_b, qk_w, qk_b, gbias, mnorm_g, w_out,
           post_g, *, tm, conv_k, qk_k):
    bsz, seq, d = h3d.shape
    n_cb = conv_w.shape[0]
    conv_ch = n_cb * LANES
    mw = M_HEADS * M_HEAD_DIM
    piece = w_rest.shape[2]
    assert w_rest.shape[0] == n_cb and piece * n_cb == 4 * mw
    assert tm % MCHUNK == 0 and seq % tm == 0 and tm >= CONV_HALO
    assert conv_k - 1 <= CONV_HALO and qk_k - 1 <= QK_HALO
    tile = pl.BlockSpec((None, tm, d), lambda b, j: (b, j, 0))
    consts = [pre_g, w_glu, w_rest, w_gate, conv_w, conv_b, ln_g, ln_b, qk_w, qk_b, gbias, mnorm_g, w_out, post_g]
    kern = functools.partial(_mixer_kernel, tm=tm, conv_ch=conv_ch, conv_k=conv_k, qk_k=qk_k)
    return pl.pallas_call(
        kern,
        out_shape=jax.ShapeDtypeStruct((bsz, seq, d), F32),
        grid=(bsz, seq // tm),
        in_specs=[tile] + [_resident(c.shape) for c in consts],
        out_specs=tile,
        scratch_shapes=[
            pltpu.VMEM((conv_ch // LANES, CONV_HALO + tm, LANES), F32),
            pltpu.VMEM((2 * mw // LANES, QK_HALO + tm, LANES), F32),
            pltpu.VMEM((M_HEADS, M_HEAD_DIM, 2 * M_HEAD_DIM), F32),
            pltpu.VMEM((8, MCHUNK), F32),
            pltpu.VMEM((tm, conv_ch + mw), BF16),
            pltpu.VMEM((tm, d), BF16),
            pltpu.VMEM((n_cb, tm, piece), F32),
            pltpu.VMEM((n_cb, tm, LANES), F32),
        ],
        compiler_params=pltpu.CompilerParams(
            dimension_semantics=("arbitrary", "arbitrary"), vmem_limit_bytes=VMEM_LIMIT_V7X),
        name="mixer",
    )(h3d, *consts)


def _memkv_kernel(mem_ref, g_ref, wk_ref, wv_ref, kt_ref, v_ref, *, scale):
    memn = _rms(mem_ref[...], g_ref[...]).astype(BF16)
    k = _dot(memn, wk_ref[...]) * scale
    kt_ref[...] = k.T.astype(BF16)
    v_ref[...] = _dot(memn, wv_ref[...]).astype(BF16)


def _memkv(mem, g, wk, wv, *, scale):
    bsz, n_mem, d = mem.shape
    return pl.pallas_call(
        functools.partial(_memkv_kernel, scale=scale),
        out_shape=(jax.ShapeDtypeStruct((bsz, d, n_mem), BF16), jax.ShapeDtypeStruct((bsz, n_mem, d), BF16)),
        grid=(bsz,),
        in_specs=[pl.BlockSpec((None, n_mem, d), lambda b: (b, 0, 0)), _resident((1, d)),
                  _resident((d, d)), _resident((d, d))],
        out_specs=(pl.BlockSpec((None, d, n_mem), lambda b: (b, 0, 0)),
                   pl.BlockSpec((None, n_mem, d), lambda b: (b, 0, 0))),
        compiler_params=pltpu.CompilerParams(
            dimension_semantics=("arbitrary",), vmem_limit_bytes=VMEM_LIMIT_V7X),
        name="memkv",
    )(mem, g, wk, wv)


def _xattn_kernel(h_ref, pre_g_ref, wq_ref, kt_ref, v_ref, wo_ref, post_g_ref, o_ref):
    h = h_ref[...]
    d = h.shape[-1]
    xd = d // X_HEADS
    hn = _rms(h, pre_g_ref[...]).astype(BF16)
    q = _dot(hn, wq_ref[...]).astype(BF16)
    outs = []
    for hh in range(X_HEADS):
        cols = slice(hh * xd, (hh + 1) * xd)
        s = _dot(q[:, cols], kt_ref[cols, :])
        p = jnp.exp(s - jnp.max(s, axis=-1, keepdims=True))
        p = p / jnp.sum(p, axis=-1, keepdims=True)
        outs.append(_dot(p.astype(BF16), v_ref[:, cols]))
    att = jnp.concatenate(outs, axis=-1).astype(BF16)
    c = _dot(att, wo_ref[...])
    o_ref[...] = h + _rms(c, post_g_ref[...])


def _xattn(h3d, pre_g, wq, kt, v, wo, post_g, *, tm):
    bsz, seq, d = h3d.shape
    n_mem = v.shape[1]
    tile = pl.BlockSpec((None, tm, d), lambda b, j: (b, j, 0))
    return pl.pallas_call(
        _xattn_kernel,
        out_shape=jax.ShapeDtypeStruct((bsz, seq, d), F32),
        grid=(bsz, seq // tm),
        in_specs=[tile, _resident((1, d)), _resident((d, d)),
                  pl.BlockSpec((None, d, n_mem), lambda b, j: (b, 0, 0)),
                  pl.BlockSpec((None, n_mem, d), lambda b, j: (b, 0, 0)),
                  _resident((d, d)), _resident((1, d))],
        out_specs=tile,
        compiler_params=pltpu.CompilerParams(
            dimension_semantics=("arbitrary", "arbitrary"), vmem_limit_bytes=VMEM_LIMIT_V7X),
        name="xattn",
    )(h3d, pre_g, wq, kt, v, wo, post_g)


def _row(x):
    return x.reshape(1, -1).astype(F32)


def _pad_rows(x, n):
    return jnp.pad(x.astype(F32), ((0, n - x.shape[0]), (0, 0)))


def kernel(x, mem, ffn1_pre_g, ffn1_w_gate, ffn1_w_up, ffn1_w_down, ffn1_post_g, mix_pre_g, w_in, conv_w, conv_b, conv_ln_g, conv_ln_b, qk_conv_w, qk_conv_b, b_igate, b_fgate, mlstm_norm_g, w_out, mix_post_g, xattn_pre_g, mem_norm_g, xattn_wq, xattn_wk, xattn_wv, xattn_wo, xattn_post_g, ffn2_pre_g, ffn2_w_gate, ffn2_w_up, ffn2_w_down, ffn2_post_g):
    bsz, seq, d = x.shape
    depth = w_in.shape[0]
    conv_k, conv_ch = conv_w.shape[1], conv_w.shape[2]
    qk_k = qk_conv_w.shape[1]
    mw = M_HEADS * M_HEAD_DIM
    main_cols = 2 * conv_ch + 4 * mw
    assert w_in.shape[2] == main_cols + 2 * M_HEADS

    h = x
    for l in range(depth):
        h = _ffn(h.reshape(bsz * seq, d), _row(ffn1_pre_g[l]), ffn1_w_gate[l].astype(BF16),
                 ffn1_w_up[l].astype(BF16), ffn1_w_down[l].astype(BF16), _row(ffn1_post_g[l]),
                 tm=512).reshape(bsz, seq, d)

        w_gate = jnp.zeros((d, 128), F32)
        w_gate = w_gate.at[:, 0:M_HEADS].set(w_in[l][:, main_cols:main_cols + M_HEADS])
        w_gate = w_gate.at[:, FGATE_LANE:FGATE_LANE + M_HEADS].set(w_in[l][:, main_cols + M_HEADS:])
        gbias = jnp.zeros((1, 128), F32)
        gbias = gbias.at[0, 0:M_HEADS].set(b_igate[l]).at[0, FGATE_LANE:FGATE_LANE + M_HEADS].set(b_fgate[l])
        n_cb = conv_ch // LANES
        w_rest = w_in[l][:, 2 * conv_ch:main_cols].astype(BF16).reshape(d, n_cb, -1).transpose(1, 0, 2)
        h = _mixer(h, _row(mix_pre_g[l]), w_in[l][:, :2 * conv_ch].astype(BF16), w_rest, w_gate.astype(BF16),
                   _lane_blocks(_pad_rows(conv_w[l], 32)), _lane_blocks(_row(conv_b[l])),
                   _row(conv_ln_g[l]), _row(conv_ln_b[l]),
                   _lane_blocks(_pad_rows(qk_conv_w[l], 8)), _lane_blocks(_row(qk_conv_b[l])),
                   gbias, _row(mlstm_norm_g[l]),
                   w_out[l].astype(BF16), _row(mix_post_g[l]), tm=512, conv_k=conv_k, qk_k=qk_k)

        kt, v = _memkv(mem, _row(mem_norm_g[l]), xattn_wk[l].astype(BF16), xattn_wv[l].astype(BF16),
                       scale=(d // X_HEADS) ** -0.5)
        h = _xattn(h, _row(xattn_pre_g[l]), xattn_wq[l].astype(BF16), kt, v, xattn_wo[l].astype(BF16),
                   _row(xattn_post_g[l]), tm=512)

        h = _ffn(h.reshape(bsz * seq, d), _row(ffn2_pre_g[l]), ffn2_w_gate[l].astype(BF16),
                 ffn2_w_up[l].astype(BF16), ffn2_w_down[l].astype(BF16), _row(ffn2_post_g[l]),
                 tm=512).reshape(bsz, seq, d)
    return h
```

```python
import functools

import jax
import jax.numpy as jnp
from jax import lax
from jax.experimental import pallas as pl
from jax.experimental.pallas import tpu as pltpu

F32 = jnp.float32
BF16 = jnp.bfloat16

RMS_EPS = 1e-6
LN_EPS = 1e-5

LANES = 128
M_HEADS = 4
M_HEAD_DIM = 128
X_HEADS = 4
CONV_HALO = 32
QK_HALO = 8
CONV_ROWS = 64
MCHUNK = 128
FGATE_LANE = 8

VMEM_LIMIT_V7X = 60 * 1024 * 1024


def _rms(x, g):
    ms = jnp.mean(x * x, axis=-1, keepdims=True)
    return x * lax.rsqrt(ms + RMS_EPS) * g


def _silu(x):
    return x * jax.nn.sigmoid(x)


def _dot(a, b):
    return jnp.dot(a, b, preferred_element_type=F32)


def _resident(shape):
    return pl.BlockSpec(shape, lambda *_: (0,) * len(shape), pipeline_mode=pl.Buffered(1))


def _ffn_kernel(h_ref, pre_g_ref, wg_ref, wu_ref, wd_ref, post_g_ref, o_ref):
    h = h_ref[...]
    hn = _rms(h, pre_g_ref[...]).astype(BF16)
    gate = _dot(hn, wg_ref[...])
    up = _dot(hn, wu_ref[...])
    a = (_silu(gate) * up).astype(BF16)
    f = _dot(a, wd_ref[...])
    o_ref[...] = h + 0.5 * _rms(f, post_g_ref[...])


def _ffn(h2d, pre_g, w_gate, w_up, w_down, post_g, *, tm):
    t, d = h2d.shape
    dff = w_gate.shape[1]
    tile = pl.BlockSpec((tm, d), lambda i: (i, 0))
    return pl.pallas_call(
        _ffn_kernel,
        out_shape=jax.ShapeDtypeStruct((t, d), F32),
        grid=(t // tm,),
        in_specs=[tile, _resident((1, d)), _resident((d, dff)), _resident((d, dff)),
                  _resident((dff, d)), _resident((1, d))],
        out_specs=tile,
        compiler_params=pltpu.CompilerParams(
            dimension_semantics=("arbitrary",), vmem_limit_bytes=VMEM_LIMIT_V7X),
        name="ffn",
    )(h2d, pre_g, w_gate, w_up, w_down, post_g)


def _dwconv_block(buf, cb, w_ref, b_ref, k, halo, tm):
    base = halo - (k - 1)
    blocks = []
    for r0 in range(0, tm, CONV_ROWS):
        acc = jnp.broadcast_to(b_ref[cb], (CONV_ROWS, LANES))
        for r in range(min(8, k)):
            taps = list(range(r, k, 8))
            strip = buf[cb, pl.ds(base + r + r0, CONV_ROWS + 8 * (len(taps) - 1)), :]
            for a, j in enumerate(taps):
                acc = acc + strip[8 * a:8 * a + CONV_ROWS, :] * w_ref[cb, j:j + 1, :]
        blocks.append(acc)
    buf[cb, 0:halo, :] = buf[cb, tm:tm + halo, :]
    return jnp.concatenate(blocks, axis=0)


def _causal_dwconv(buf, x, w_ref, b_ref, k, halo):
    tm, ch = x.shape
    cols = []
    for cb in range(ch // LANES):
        buf[cb, halo:halo + tm, :] = x[:, cb * LANES:(cb + 1) * LANES]
        cols.append(_dwconv_block(buf, cb, w_ref, b_ref, k, halo, tm))
    return jnp.concatenate(cols, axis=1)


def _mixer_kernel(h_ref, pre_g_ref, w_glu_ref, w_rest_ref, w_gate_ref, conv_w_ref, conv_b_ref, ln_g_ref, ln_b_ref,
                  qk_w_ref, qk_b_ref, gbias_ref, mnorm_g_ref, w_out_ref, post_g_ref, o_ref,
                  ubuf, qkbuf, cext, mstate, mixin, hn_scr, zrest, convout, *, tm, conv_ch, conv_k, qk_k):
    mw = M_HEADS * M_HEAD_DIM
    hd = M_HEAD_DIM

    @pl.when(pl.program_id(1) == 0)
    def _():
        ubuf[:, 0:CONV_HALO, :] = jnp.zeros((conv_ch // LANES, CONV_HALO, LANES), F32)
        qkbuf[:, 0:QK_HALO, :] = jnp.zeros((2 * mw // LANES, QK_HALO, LANES), F32)
        cext[...] = jnp.zeros(cext.shape, F32)
        mstate[...] = jnp.zeros(mstate.shape, F32)

    h = h_ref[...]
    hn_scr[...] = _rms(h, pre_g_ref[...]).astype(BF16)
    za = _dot(hn_scr[...], w_glu_ref[...])
    zg = _dot(hn_scr[...], w_gate_ref[...])

    u = za[:, 0:conv_ch] * jax.nn.sigmoid(za[:, conv_ch:2 * conv_ch])
    n_cb = conv_ch // LANES
    for cb in range(n_cb):
        ubuf[cb, CONV_HALO:CONV_HALO + tm, :] = u[:, cb * LANES:(cb + 1) * LANES]

    def conv_and_project(cb, carry):
        zrest[cb] = _dot(hn_scr[...], w_rest_ref[cb])
        convout[cb] = _dwconv_block(ubuf, cb, conv_w_ref, conv_b_ref, conv_k, CONV_HALO, tm)
        return carry

    lax.fori_loop(0, n_cb, conv_and_project, 0)
    acc = jnp.concatenate([convout[cb] for cb in range(n_cb)], axis=1)
    z = jnp.concatenate([zrest[p] for p in range(n_cb)], axis=1)
    mu = jnp.mean(acc, axis=-1, keepdims=True)
    cen = acc - mu
    var = jnp.mean(cen * cen, axis=-1, keepdims=True)
    yn = cen * lax.rsqrt(var + LN_EPS) * ln_g_ref[...] + ln_b_ref[...]
    mixin[:, 0:conv_ch] = _silu(yn).astype(BF16)

    qk = _silu(_causal_dwconv(qkbuf, z[:, 0:2 * mw], qk_w_ref, qk_b_ref, qk_k, QK_HALO))
    q = qk[:, 0:mw].astype(BF16)
    k = qk[:, mw:2 * mw] * (hd ** -0.5)
    v = z[:, 2 * mw:3 * mw].astype(BF16)
    og = jax.nn.sigmoid(z[:, 3 * mw:4 * mw])

    zgb = zg + gbias_ref[...]
    log_f = jnp.minimum(zgb, 0.0) - jnp.log1p(jnp.exp(-jnp.abs(zgb)))
    lane = lax.broadcasted_iota(jnp.int32, zgb.shape, 1)
    gates_t = jnp.where(lane >= FGATE_LANE, log_f, zgb).T
    li_all = gates_t[0:8, :]
    lf_all = gates_t[FGATE_LANE:FGATE_LANE + 8, :]

    row_i = lax.broadcasted_iota(jnp.int32, (MCHUNK, MCHUNK), 0)
    col_i = lax.broadcasted_iota(jnp.int32, (MCHUNK, MCHUNK), 1)
    causal = col_i <= row_i
    lane8 = lax.broadcasted_iota(jnp.int32, (8, MCHUNK), 1)
    ones_col = (lax.broadcasted_iota(jnp.int32, (MCHUNK, hd), 1) == 0).astype(BF16)

    for c in range(tm // MCHUNK):
        rows = slice(c * MCHUNK, (c + 1) * MCHUNK)
        li = li_all[:, rows]
        lf = lf_all[:, rows]
        b = lf
        for d in (1, 2, 4, 8, 16, 32, 64):
            b = b + jnp.where(lane8 >= d, pltpu.roll(b, d, axis=1), 0.0)
        b_end = jnp.sum(lf, axis=1, keepdims=True)
        m_prev = mstate[:, 0:1]
        g = b_end - b + li
        m_new = jnp.maximum(b_end + m_prev, jnp.max(g, axis=1, keepdims=True))
        decay = jnp.exp(b_end + m_prev - m_new)
        wg = jnp.exp(g - m_new)
        mstate[...] = jnp.broadcast_to(m_new, mstate.shape)

        for hh in range(M_HEADS):
            cols = slice(hh * hd, (hh + 1) * hd)
            qh = q[rows, cols]
            kt = k[rows, cols].T
            vh = v[rows, cols]
            b_col = jnp.sum(jnp.where(causal, lf[hh:hh + 1, :], 0.0), axis=1, keepdims=True)
            dmat = jnp.where(causal, b_col - b[hh:hh + 1, :] + li[hh:hh + 1, :], -jnp.inf)
            inter = b_col + m_prev[hh:hh + 1, :]
            m_t = jnp.maximum(inter, jnp.max(dmat, axis=1, keepdims=True))
            s = _dot(qh, kt.astype(BF16))
            w = jnp.exp(dmat - m_t) * s
            a = jnp.exp(inter - m_t)
            ce = cext[hh]
            qc = _dot(qh, ce.astype(BF16))
            num = _dot(w.astype(BF16), vh) + a * qc[:, 0:hd]
            den = jnp.sum(w, axis=1, keepdims=True) + a * qc[:, hd:hd + 1]
            hv = num / jnp.maximum(jnp.abs(den), jnp.exp(-m_t))
            hv = hv * lax.rsqrt(jnp.mean(hv * hv, axis=-1, keepdims=True) + RMS_EPS)
            hv = hv * mnorm_g_ref[:, cols]
            mixin[rows, conv_ch + hh * hd:conv_ch + (hh + 1) * hd] = (og[rows, cols] * hv).astype(BF16)

            ktw = (kt * wg[hh:hh + 1, :]).astype(BF16)
            vext = jnp.concatenate([vh, ones_col], axis=1)
            cext[hh] = decay[hh:hh + 1, :] * ce + _dot(ktw, vext)

    m = _dot(mixin[...], w_out_ref[...])
    o_ref[...] = h + _rms(m, post_g_ref[...])


def _lane_blocks(x):
    rows, ch = x.shape
    return x.astype(F32).reshape(rows, ch // LANES, LANES).transpose(1, 0, 2)


def _mixer(h3d, pre_g, w_glu, w_rest, w_gate, conv_w, conv_b, ln_g, ln_b, qk_w, qk_b, gbias, mnorm_g, w_out,
           post_g, *, tm, conv_k, qk_k):
    bsz, seq, d = h3d.shape
    n_cb = conv_w.shape[0]
    conv_ch = n_cb * LANES
    mw = M_HEADS * M_HEAD_DIM
    piece = w_rest.shape[2]
    assert w_rest.shape[0] == n_cb and piece * n_cb == 4 * mw
    assert tm % MCHUNK == 0 and seq % tm == 0 and tm >= CONV_HALO
    assert conv_k - 1 <= CONV_HALO and qk_k - 1 <= QK_HALO
    tile = pl.BlockSpec((None, tm, d), lambda b, j: (b, j, 0))
    consts = [pre_g, w_glu, w_rest, w_gate, conv_w, conv_b, ln_g, ln_b, qk_w, qk_b, gbias, mnorm_g, w_out, post_g]
    kern = functools.partial(_mixer_kernel, tm=tm, conv_ch=conv_ch, conv_k=conv_k, qk_k=qk_k)
    return pl.pallas_call(
        kern,
        out_shape=jax.ShapeDtypeStruct((bsz, seq, d), F32),
        grid=(bsz, seq // tm),
        in_specs=[tile] + [_resident(c.shape) for c in consts],
        out_specs=tile,
        scratch_shapes=[
            pltpu.VMEM((conv_ch // LANES, CONV_HALO + tm, LANES), F32),
            pltpu.VMEM((2 * mw // LANES, QK_HALO + tm, LANES), F32),
            pltpu.VMEM((M_HEADS, M_HEAD_DIM, 2 * M_HEAD_DIM), F32),
            pltpu.VMEM((8, MCHUNK), F32),
            pltpu.VMEM((tm, conv_ch + mw), BF16),
            pltpu.VMEM((tm, d), BF16),
            pltpu.VMEM((n_cb, tm, piece), F32),
            pltpu.VMEM((n_cb, tm, LANES), F32),
        ],
        compiler_params=pltpu.CompilerParams(
            dimension_semantics=("arbitrary", "arbitrary"), vmem_limit_bytes=VMEM_LIMIT_V7X),
        name="mixer",
    )(h3d, *consts)


def _memkv_kernel(mem_ref, g_ref, wk_ref, wv_ref, kt_ref, v_ref, *, scale):
    memn = _rms(mem_ref[...], g_ref[...]).astype(BF16)
    k = _dot(memn, wk_ref[...]) * scale
    kt_ref[...] = k.T.astype(BF16)
    v_ref[...] = _dot(memn, wv_ref[...]).astype(BF16)


def _memkv(mem, g, wk, wv, *, scale):
    bsz, n_mem, d = mem.shape
    return pl.pallas_call(
        functools.partial(_memkv_kernel, scale=scale),
        out_shape=(jax.ShapeDtypeStruct((bsz, d, n_mem), BF16), jax.ShapeDtypeStruct((bsz, n_mem, d), BF16)),
        grid=(bsz,),
        in_specs=[pl.BlockSpec((None, n_mem, d), lambda b: (b, 0, 0)), _resident((1, d)),
                  _resident((d, d)), _resident((d, d))],
        out_specs=(pl.BlockSpec((None, d, n_mem), lambda b: (b, 0, 0)),
                   pl.BlockSpec((None, n_mem, d), lambda b: (b, 0, 0))),
        compiler_params=pltpu.CompilerParams(
            dimension_semantics=("arbitrary",), vmem_limit_bytes=VMEM_LIMIT_V7X),
        name="memkv",
    )(mem, g, wk, wv)


def _xattn_kernel(h_ref, pre_g_ref, wq_ref, kt_ref, v_ref, wo_ref, post_g_ref, o_ref):
    h = h_ref[...]
    d = h.shape[-1]
    xd = d // X_HEADS
    hn = _rms(h, pre_g_ref[...]).astype(BF16)
    q = _dot(hn, wq_ref[...]).astype(BF16)
    outs = []
    for hh in range(X_HEADS):
        cols = slice(hh * xd, (hh + 1) * xd)
        s = _dot(q[:, cols], kt_ref[cols, :])
        p = jnp.exp(s - jnp.max(s, axis=-1, keepdims=True))
        p = p / jnp.sum(p, axis=-1, keepdims=True)
        outs.append(_dot(p.astype(BF16), v_ref[:, cols]))
    att = jnp.concatenate(outs, axis=-1).astype(BF16)
    c = _dot(att, wo_ref[...])
    o_ref[...] = h + _rms(c, post_g_ref[...])


def _xattn(h3d, pre_g, wq, kt, v, wo, post_g, *, tm):
    bsz, seq, d = h3d.shape
    n_mem = v.shape[1]
    tile = pl.BlockSpec((None, tm, d), lambda b, j: (b, j, 0))
    return pl.pallas_call(
        _xattn_kernel,
        out_shape=jax.ShapeDtypeStruct((bsz, seq, d), F32),
        grid=(bsz, seq // tm),
        in_specs=[tile, _resident((1, d)), _resident((d, d)),
                  pl.BlockSpec((None, d, n_mem), lambda b, j: (b, 0, 0)),
                  pl.BlockSpec((None, n_mem, d), lambda b, j: (b, 0, 0)),
                  _resident((d, d)), _resident((1, d))],
        out_specs=tile,
        compiler_params=pltpu.CompilerParams(
            dimension_semantics=("arbitrary", "arbitrary"), vmem_limit_bytes=VMEM_LIMIT_V7X),
        name="xattn",
    )(h3d, pre_g, wq, kt, v, wo, post_g)


def _row(x):
    return x.reshape(1, -1).astype(F32)


def _pad_rows(x, n):
    return jnp.pad(x.astype(F32), ((0, n - x.shape[0]), (0, 0)))


def kernel(x, mem, ffn1_pre_g, ffn1_w_gate, ffn1_w_up, ffn1_w_down, ffn1_post_g, mix_pre_g, w_in, conv_w, conv_b, conv_ln_g, conv_ln_b, qk_conv_w, qk_conv_b, b_igate, b_fgate, mlstm_norm_g, w_out, mix_post_g, xattn_pre_g, mem_norm_g, xattn_wq, xattn_wk, xattn_wv, xattn_wo, xattn_post_g, ffn2_pre_g, ffn2_w_gate, ffn2_w_up, ffn2_w_down, ffn2_post_g):
    bsz, seq, d = x.shape
    depth = w_in.shape[0]
    conv_k, conv_ch = conv_w.shape[1], conv_w.shape[2]
    qk_k = qk_conv_w.shape[1]
    mw = M_HEADS * M_HEAD_DIM
    main_cols = 2 * conv_ch + 4 * mw
    assert w_in.shape[2] == main_cols + 2 * M_HEADS

    h = x
    for l in range(depth):
        h = _ffn(h.reshape(bsz * seq, d), _row(ffn1_pre_g[l]), ffn1_w_gate[l].astype(BF16),
                 ffn1_w_up[l].astype(BF16), ffn1_w_down[l].astype(BF16), _row(ffn1_post_g[l]),
                 tm=512).reshape(bsz, seq, d)

        w_gate = jnp.zeros((d, 128), F32)
        w_gate = w_gate.at[:, 0:M_HEADS].set(w_in[l][:, main_cols:main_cols + M_HEADS])
        w_gate = w_gate.at[:, FGATE_LANE:FGATE_LANE + M_HEADS].set(w_in[l][:, main_cols + M_HEADS:])
        gbias = jnp.zeros((1, 128), F32)
        gbias = gbias.at[0, 0:M_HEADS].set(b_igate[l]).at[0, FGATE_LANE:FGATE_LANE + M_HEADS].set(b_fgate[l])
        n_cb = conv_ch // LANES
        w_rest = w_in[l][:, 2 * conv_ch:main_cols].astype(BF16).reshape(d, n_cb, -1).transpose(1, 0, 2)
        h = _mixer(h, _row(mix_pre_g[l]), w_in[l][:, :2 * conv_ch].astype(BF16), w_rest, w_gate.astype(BF16),
                   _lane_blocks(_pad_rows(conv_w[l], 32)), _lane_blocks(_row(conv_b[l])),
                   _row(conv_ln_g[l]), _row(conv_ln_b[l]),
                   _lane_blocks(_pad_rows(qk_conv_w[l], 8)), _lane_blocks(_row(qk_conv_b[l])),
                   gbias, _row(mlstm_norm_g[l]),
                   w_out[l].astype(BF16), _row(mix_post_g[l]), tm=512, conv_k=conv_k, qk_k=qk_k)

        kt, v = _memkv(mem, _row(mem_norm_g[l]), xattn_wk[l].astype(BF16), xattn_wv[l].astype(BF16),
                       scale=(d // X_HEADS) ** -0.5)
        h = _xattn(h, _row(xattn_pre_g[l]), xattn_wq[l].astype(BF16), kt, v, xattn_wo[l].astype(BF16),
                   _row(xattn_post_g[l]), tm=512)

        h = _ffn(h.reshape(bsz * seq, d), _row(ffn2_pre_g[l]), ffn2_w_gate[l].astype(BF16),
                 ffn2_w_up[l].astype(BF16), ffn2_w_down[l].astype(BF16), _row(ffn2_post_g[l]),
                 tm=512).reshape(bsz, seq, d)
    return h
```

```python
import functools

import jax
import jax.numpy as jnp
from jax import lax
from jax.experimental import pallas as pl
from jax.experimental.pallas import tpu as pltpu

F32 = jnp.float32
BF16 = jnp.bfloat16

RMS_EPS = 1e-6
LN_EPS = 1e-5

LANES = 128
MXU_COLS = 256
M_HEADS = 4
M_HEAD_DIM = 128
X_HEADS = 4
CONV_HALO = 32
QK_HALO = 8
MCHUNK = 128
FGATE_LANE = 8
TOKEN_TILE = 1024

VMEM_LIMIT_V7X = 60 * 1024 * 1024


def _rms(x, g):
    ms = jnp.mean(x * x, axis=-1, keepdims=True)
    return x * lax.rsqrt(ms + RMS_EPS) * g


def _silu(x):
    return x * jax.nn.sigmoid(x)


def _dot(a, b):
    return jnp.dot(a, b, preferred_element_type=F32)


def _resident(shape):
    return pl.BlockSpec(shape, lambda *_: (0,) * len(shape), pipeline_mode=pl.Buffered(1))


def _ffn_kernel(h_ref, pre_g_ref, wg_ref, wu_ref, wd_ref, post_g_ref, o_ref):
    h = h_ref[...]
    hn = _rms(h, pre_g_ref[...]).astype(BF16)
    dff = wg_ref.shape[1]
    split = pl.cdiv(dff // MXU_COLS, 2) * MXU_COLS
    f = None
    for lo, hi in ((0, split), (split, dff)):
        gate = _dot(hn, wg_ref[:, lo:hi])
        up = _dot(hn, wu_ref[:, lo:hi])
        a = (_silu(gate) * up).astype(BF16)
        part = _dot(a, wd_ref[lo:hi, :])
        f = part if f is None else f + part
    o_ref[...] = h + 0.5 * _rms(f, post_g_ref[...])


def _ffn(h2d, pre_g, w_gate, w_up, w_down, post_g, *, tm):
    t, d = h2d.shape
    dff = w_gate.shape[1]
    assert t % tm == 0 and dff % MXU_COLS == 0
    tile = pl.BlockSpec((tm, d), lambda i: (i, 0))
    return pl.pallas_call(
        _ffn_kernel,
        out_shape=jax.ShapeDtypeStruct((t, d), F32),
        grid=(t // tm,),
        in_specs=[tile, _resident((1, d)), _resident((d, dff)), _resident((d, dff)),
                  _resident((dff, d)), _resident((1, d))],
        out_specs=tile,
        compiler_params=pltpu.CompilerParams(
            dimension_semantics=("arbitrary",), vmem_limit_bytes=VMEM_LIMIT_V7X),
        name="ffn",
    )(h2d, pre_g, w_gate, w_up, w_down, post_g)


def _causal_dwconv(buf, x, w_ref, b_ref, k, halo):
    tm, ch = x.shape
    base = halo - (k - 1)
    cols = []
    for cb in range(ch // LANES):
        lanes = slice(cb * LANES, (cb + 1) * LANES)
        buf[cb, halo:halo + tm, :] = x[:, lanes]
        acc = jnp.broadcast_to(b_ref[:, lanes], (tm, LANES))
        for j in range(k):
            acc = acc + buf[cb, pl.ds(base + j, tm), :] * w_ref[j:j + 1, lanes]
        buf[cb, 0:halo, :] = buf[cb, tm:tm + halo, :]
        cols.append(acc)
    return jnp.concatenate(cols, axis=1)


def _mixer_kernel(h_ref, pre_g_ref, w_main_ref, w_gate_ref, conv_w_ref, conv_b_ref, ln_g_ref, ln_b_ref,
                  qk_w_ref, qk_b_ref, gbias_ref, mnorm_g_ref, w_out_ref, post_g_ref, o_ref,
                  ubuf, qkbuf, cext, mstate, mixin, *, tm, conv_ch, conv_k, qk_k):
    mw = M_HEADS * M_HEAD_DIM
    hd = M_HEAD_DIM

    @pl.when(pl.program_id(1) == 0)
    def _():
        ubuf[:, 0:CONV_HALO, :] = jnp.zeros((conv_ch // LANES, CONV_HALO, LANES), F32)
        qkbuf[:, 0:QK_HALO, :] = jnp.zeros((2 * mw // LANES, QK_HALO, LANES), F32)
        cext[...] = jnp.zeros(cext.shape, F32)
        mstate[...] = jnp.zeros(mstate.shape, F32)

    h = h_ref[...]
    hn = _rms(h, pre_g_ref[...]).astype(BF16)
    z = _dot(hn, w_main_ref[...])
    zg = _dot(hn, w_gate_ref[...])

    u = z[:, 0:conv_ch] * jax.nn.sigmoid(z[:, conv_ch:2 * conv_ch])
    acc = _causal_dwconv(ubuf, u, conv_w_ref, conv_b_ref, conv_k, CONV_HALO)
    mu = jnp.mean(acc, axis=-1, keepdims=True)
    cen = acc - mu
    var = jnp.mean(cen * cen, axis=-1, keepdims=True)
    yn = cen * lax.rsqrt(var + LN_EPS) * ln_g_ref[...] + ln_b_ref[...]
    mixin[:, 0:conv_ch] = _silu(yn).astype(BF16)

    qk = _silu(_causal_dwconv(qkbuf, z[:, 2 * conv_ch:2 * conv_ch + 2 * mw], qk_w_ref, qk_b_ref, qk_k, QK_HALO))
    q = qk[:, 0:mw].astype(BF16)
    k = qk[:, mw:2 * mw] * (hd ** -0.5)
    v = z[:, 2 * conv_ch + 2 * mw:2 * conv_ch + 3 * mw].astype(BF16)
    og = jax.nn.sigmoid(z[:, 2 * conv_ch + 3 * mw:2 * conv_ch + 4 * mw])

    zgb = zg + gbias_ref[...]
    log_f = jnp.minimum(zgb, 0.0) - jnp.log1p(jnp.exp(-jnp.abs(zgb)))
    lane = lax.broadcasted_iota(jnp.int32, zgb.shape, 1)
    gates_t = jnp.where(lane >= FGATE_LANE, log_f, zgb).T
    li_all = gates_t[0:8, :]
    lf_all = gates_t[FGATE_LANE:FGATE_LANE + 8, :]

    row_i = lax.broadcasted_iota(jnp.int32, (MCHUNK, MCHUNK), 0)
    col_i = lax.broadcasted_iota(jnp.int32, (MCHUNK, MCHUNK), 1)
    causal = col_i <= row_i
    lane8 = lax.broadcasted_iota(jnp.int32, (8, MCHUNK), 1)
    ones_col = (lax.broadcasted_iota(jnp.int32, (MCHUNK, hd), 1) == 0).astype(BF16)

    for c in range(tm // MCHUNK):
        rows = slice(c * MCHUNK, (c + 1) * MCHUNK)
        li = li_all[:, rows]
        lf = lf_all[:, rows]
        b = lf
        for d in (1, 2, 4, 8, 16, 32, 64):
            b = b + jnp.where(lane8 >= d, pltpu.roll(b, d, axis=1), 0.0)
        b_end = jnp.sum(lf, axis=1, keepdims=True)
        m_prev = mstate[:, 0:1]
        g = b_end - b + li
        m_new = jnp.maximum(b_end + m_prev, jnp.max(g, axis=1, keepdims=True))
        decay = jnp.exp(b_end + m_prev - m_new)
        wg = jnp.exp(g - m_new)
        mstate[...] = jnp.broadcast_to(m_new, mstate.shape)

        for hh in range(M_HEADS):
            cols = slice(hh * hd, (hh + 1) * hd)
            qh = q[rows, cols]
            kt = k[rows, cols].T
            vh = v[rows, cols]
            b_col = jnp.sum(jnp.where(causal, lf[hh:hh + 1, :], 0.0), axis=1, keepdims=True)
            dmat = jnp.where(causal, b_col - b[hh:hh + 1, :] + li[hh:hh + 1, :], -jnp.inf)
            inter = b_col + m_prev[hh:hh + 1, :]
            m_t = jnp.maximum(inter, jnp.max(dmat, axis=1, keepdims=True))
            s = _dot(qh, kt.astype(BF16))
            w = jnp.exp(dmat - m_t) * s
            a = jnp.exp(inter - m_t)
            ce = cext[hh]
            qc = _dot(qh, ce.astype(BF16))
            num = _dot(w.astype(BF16), vh) + a * qc[:, 0:hd]
            den = jnp.sum(w, axis=1, keepdims=True) + a * qc[:, hd:hd + 1]
            hv = num / jnp.maximum(jnp.abs(den), jnp.exp(-m_t))
            hv = hv * lax.rsqrt(jnp.mean(hv * hv, axis=-1, keepdims=True) + RMS_EPS)
            hv = hv * mnorm_g_ref[:, cols]
            mixin[rows, conv_ch + hh * hd:conv_ch + (hh + 1) * hd] = (og[rows, cols] * hv).astype(BF16)

            ktw = (kt * wg[hh:hh + 1, :]).astype(BF16)
            vext = jnp.concatenate([vh, ones_col], axis=1)
            cext[hh] = decay[hh:hh + 1, :] * ce + _dot(ktw, vext)

    m = _dot(mixin[...], w_out_ref[...])
    o_ref[...] = h + _rms(m, post_g_ref[...])


def _mixer(h3d, pre_g, w_main, w_gate, conv_w, conv_b, ln_g, ln_b, qk_w, qk_b, gbias, mnorm_g, w_out, post_g,
           *, tm, conv_k, qk_k):
    bsz, seq, d = h3d.shape
    conv_ch = conv_w.shape[1]
    mw = M_HEADS * M_HEAD_DIM
    assert tm % MCHUNK == 0 and seq % tm == 0 and tm >= CONV_HALO
    assert conv_k - 1 <= CONV_HALO and qk_k - 1 <= QK_HALO
    tile = pl.BlockSpec((None, tm, d), lambda b, j: (b, j, 0))
    consts = [pre_g, w_main, w_gate, conv_w, conv_b, ln_g, ln_b, qk_w, qk_b, gbias, mnorm_g, w_out, post_g]
    kern = functools.partial(_mixer_kernel, tm=tm, conv_ch=conv_ch, conv_k=conv_k, qk_k=qk_k)
    return pl.pallas_call(
        kern,
        out_shape=jax.ShapeDtypeStruct((bsz, seq, d), F32),
        grid=(bsz, seq // tm),
        in_specs=[tile] + [_resident(c.shape) for c in consts],
        out_specs=tile,
        scratch_shapes=[
            pltpu.VMEM((conv_ch // LANES, CONV_HALO + tm, LANES), F32),
            pltpu.VMEM((2 * mw // LANES, QK_HALO + tm, LANES), F32),
            pltpu.VMEM((M_HEADS, M_HEAD_DIM, 2 * M_HEAD_DIM), F32),
            pltpu.VMEM((8, MCHUNK), F32),
            pltpu.VMEM((tm, conv_ch + mw), BF16),
        ],
        compiler_params=pltpu.CompilerParams(
            dimension_semantics=("arbitrary", "arbitrary"), vmem_limit_bytes=VMEM_LIMIT_V7X),
        name="mixer",
    )(h3d, *consts)


def _memkv_kernel(mem_ref, g_ref, wk_ref, wv_ref, kt_ref, v_ref, *, scale):
    memn = _rms(mem_ref[...], g_ref[...]).astype(BF16)
    k = _dot(memn, wk_ref[...]) * scale
    kt_ref[...] = k.T.astype(BF16)
    v_ref[...] = _dot(memn, wv_ref[...]).astype(BF16)


def _memkv(mem, g, wk, wv, *, scale):
    bsz, n_mem, d = mem.shape
    return pl.pallas_call(
        functools.partial(_memkv_kernel, scale=scale),
        out_shape=(jax.ShapeDtypeStruct((bsz, d, n_mem), BF16), jax.ShapeDtypeStruct((bsz, n_mem, d), BF16)),
        grid=(bsz,),
        in_specs=[pl.BlockSpec((None, n_mem, d), lambda b: (b, 0, 0)), _resident((1, d)),
                  _resident((d, d)), _resident((d, d))],
        out_specs=(pl.BlockSpec((None, d, n_mem), lambda b: (b, 0, 0)),
                   pl.BlockSpec((None, n_mem, d), lambda b: (b, 0, 0))),
        compiler_params=pltpu.CompilerParams(
            dimension_semantics=("arbitrary",), vmem_limit_bytes=VMEM_LIMIT_V7X),
        name="memkv",
    )(mem, g, wk, wv)


def _xattn_kernel(h_ref, pre_g_ref, wq_ref, kt_ref, v_ref, wo_ref, post_g_ref, o_ref):
    h = h_ref[...]
    d = h.shape[-1]
    xd = d // X_HEADS
    hn = _rms(h, pre_g_ref[...]).astype(BF16)
    q = _dot(hn, wq_ref[...]).astype(BF16)
    outs = []
    for hh in range(X_HEADS):
        cols = slice(hh * xd, (hh + 1) * xd)
        s = _dot(q[:, cols], kt_ref[cols, :])
        p = jnp.exp(s - jnp.max(s, axis=-1, keepdims=True))
        p = p / jnp.sum(p, axis=-1, keepdims=True)
        outs.append(_dot(p.astype(BF16), v_ref[:, cols]))
    att = jnp.concatenate(outs, axis=-1).astype(BF16)
    c = _dot(att, wo_ref[...])
    o_ref[...] = h + _rms(c, post_g_ref[...])


def _xattn(h3d, pre_g, wq, kt, v, wo, post_g, *, tm):
    bsz, seq, d = h3d.shape
    n_mem = v.shape[1]
    assert seq % tm == 0
    tile = pl.BlockSpec((None, tm, d), lambda b, j: (b, j, 0))
    return pl.pallas_call(
        _xattn_kernel,
        out_shape=jax.ShapeDtypeStruct((bsz, seq, d), F32),
        grid=(bsz, seq // tm),
        in_specs=[tile, _resident((1, d)), _resident((d, d)),
                  pl.BlockSpec((None, d, n_mem), lambda b, j: (b, 0, 0)),
                  pl.BlockSpec((None, n_mem, d), lambda b, j: (b, 0, 0)),
                  _resident((d, d)), _resident((1, d))],
        out_specs=tile,
        compiler_params=pltpu.CompilerParams(
            dimension_semantics=("arbitrary", "arbitrary"), vmem_limit_bytes=VMEM_LIMIT_V7X),
        name="xattn",
    )(h3d, pre_g, wq, kt, v, wo, post_g)


def _row(x):
    return x.reshape(1, -1).astype(F32)


def _pad_rows(x, n):
    return jnp.pad(x.astype(F32), ((0, n - x.shape[0]), (0, 0)))


def kernel(x, mem, ffn1_pre_g, ffn1_w_gate, ffn1_w_up, ffn1_w_down, ffn1_post_g, mix_pre_g, w_in, conv_w, conv_b, conv_ln_g, conv_ln_b, qk_conv_w, qk_conv_b, b_igate, b_fgate, mlstm_norm_g, w_out, mix_post_g, xattn_pre_g, mem_norm_g, xattn_wq, xattn_wk, xattn_wv, xattn_wo, xattn_post_g, ffn2_pre_g, ffn2_w_gate, ffn2_w_up, ffn2_w_down, ffn2_post_g):
    bsz, seq, d = x.shape
    depth = w_in.shape[0]
    conv_k, conv_ch = conv_w.shape[1], conv_w.shape[2]
    qk_k = qk_conv_w.shape[1]
    mw = M_HEADS * M_HEAD_DIM
    main_cols = 2 * conv_ch + 4 * mw
    assert w_in.shape[2] == main_cols + 2 * M_HEADS
    tm = TOKEN_TILE

    h = x
    for l in range(depth):
        h = _ffn(h.reshape(bsz * seq, d), _row(ffn1_pre_g[l]), ffn1_w_gate[l].astype(BF16),
                 ffn1_w_up[l].astype(BF16), ffn1_w_down[l].astype(BF16), _row(ffn1_post_g[l]),
                 tm=tm).reshape(bsz, seq, d)

        w_gate = jnp.zeros((d, LANES), F32)
        w_gate = w_gate.at[:, 0:M_HEADS].set(w_in[l][:, main_cols:main_cols + M_HEADS])
        w_gate = w_gate.at[:, FGATE_LANE:FGATE_LANE + M_HEADS].set(w_in[l][:, main_cols + M_HEADS:])
        gbias = jnp.zeros((1, LANES), F32)
        gbias = gbias.at[0, 0:M_HEADS].set(b_igate[l]).at[0, FGATE_LANE:FGATE_LANE + M_HEADS].set(b_fgate[l])
        h = _mixer(h, _row(mix_pre_g[l]), w_in[l][:, :main_cols].astype(BF16), w_gate.astype(BF16),
                   _pad_rows(conv_w[l], 32), _row(conv_b[l]), _row(conv_ln_g[l]), _row(conv_ln_b[l]),
                   _pad_rows(qk_conv_w[l], 8), _row(qk_conv_b[l]), gbias, _row(mlstm_norm_g[l]),
                   w_out[l].astype(BF16), _row(mix_post_g[l]), tm=tm, conv_k=conv_k, qk_k=qk_k)

        kt, v = _memkv(mem, _row(mem_norm_g[l]), xattn_wk[l].astype(BF16), xattn_wv[l].astype(BF16),
                       scale=(d // X_HEADS) ** -0.5)
        h = _xattn(h, _row(xattn_pre_g[l]), xattn_wq[l].astype(BF16), kt, v, xattn_wo[l].astype(BF16),
                   _row(xattn_post_g[l]), tm=tm)

        h = _ffn(h.reshape(bsz * seq, d), _row(ffn2_pre_g[l]), ffn2_w_gate[l].astype(BF16),
                 ffn2_w_up[l].astype(BF16), ffn2_w_down[l].astype(BF16), _row(ffn2_post_g[l]),
                 tm=tm).reshape(bsz, seq, d)
    return h
```

```python
import functools

import jax
import jax.numpy as jnp
from jax import lax
from jax.experimental import pallas as pl
from jax.experimental.pallas import tpu as pltpu

F32 = jnp.float32
BF16 = jnp.bfloat16

RMS_EPS = 1e-6
LN_EPS = 1e-5

LANES = 128
MXU_COLS = 256
M_HEADS = 4
M_HEAD_DIM = 128
X_HEADS = 4
CONV_HALO = 32
QK_HALO = 8
MCHUNK = 128
FGATE_LANE = 8
TOKEN_TILE = 1024

VMEM_LIMIT_V7X = 60 * 1024 * 1024


def _rms(x, g):
    ms = jnp.mean(x * x, axis=-1, keepdims=True)
    return x * lax.rsqrt(ms + RMS_EPS) * g


def _sigmoid(x):
    return 0.5 * jnp.tanh(0.5 * x) + 0.5


def _silu(x):
    return x * _sigmoid(x)


def _dot(a, b):
    return jnp.dot(a, b, preferred_element_type=F32)


def _resident(shape):
    return pl.BlockSpec(shape, lambda *_: (0,) * len(shape), pipeline_mode=pl.Buffered(1))


def _ffn_kernel(h_ref, pre_g_ref, wg_ref, wu_ref, wd_ref, post_g_ref, o_ref):
    h = h_ref[...]
    hn = _rms(h, pre_g_ref[...]).astype(BF16)
    dff = wg_ref.shape[1]
    split = pl.cdiv(dff // MXU_COLS, 2) * MXU_COLS
    f = None
    for lo, hi in ((0, split), (split, dff)):
        gate = _dot(hn, wg_ref[:, lo:hi])
        up = _dot(hn, wu_ref[:, lo:hi])
        a = (_silu(gate) * up).astype(BF16)
        part = _dot(a, wd_ref[lo:hi, :])
        f = part if f is None else f + part
    o_ref[...] = h + 0.5 * _rms(f, post_g_ref[...])


def _ffn(h2d, pre_g, w_gate, w_up, w_down, post_g, *, tm):
    t, d = h2d.shape
    dff = w_gate.shape[1]
    assert t % tm == 0 and dff % MXU_COLS == 0
    tile = pl.BlockSpec((tm, d), lambda i: (i, 0))
    return pl.pallas_call(
        _ffn_kernel,
        out_shape=jax.ShapeDtypeStruct((t, d), F32),
        grid=(t // tm,),
        in_specs=[tile, _resident((1, d)), _resident((d, dff)), _resident((d, dff)),
                  _resident((dff, d)), _resident((1, d))],
        out_specs=tile,
        compiler_params=pltpu.CompilerParams(
            dimension_semantics=("arbitrary",), vmem_limit_bytes=VMEM_LIMIT_V7X),
        name="ffn",
    )(h2d, pre_g, w_gate, w_up, w_down, post_g)


def _causal_dwconv(buf, x, w_ref, b_ref, k, halo):
    tm, ch = x.shape
    base = halo - (k - 1)
    cols = []
    for cb in range(ch // LANES):
        lanes = slice(cb * LANES, (cb + 1) * LANES)
        buf[cb, halo:halo + tm, :] = x[:, lanes]
        acc = jnp.broadcast_to(b_ref[:, lanes], (tm, LANES))
        for j in range(k):
            acc = acc + buf[cb, pl.ds(base + j, tm), :] * w_ref[j:j + 1, lanes]
        buf[cb, 0:halo, :] = buf[cb, tm:tm + halo, :]
        cols.append(acc)
    return jnp.concatenate(cols, axis=1)


def _mixer_kernel(h_ref, pre_g_ref, w_main_ref, w_gate_ref, conv_w_ref, conv_b_ref, ln_g_ref, ln_b_ref,
                  qk_w_ref, qk_b_ref, gbias_ref, mnorm_g_ref, w_out_ref, post_g_ref, o_ref,
                  ubuf, qkbuf, cext, mstate, mixin, *, tm, conv_ch, conv_k, qk_k):
    mw = M_HEADS * M_HEAD_DIM
    hd = M_HEAD_DIM

    @pl.when(pl.program_id(1) == 0)
    def _():
        ubuf[:, 0:CONV_HALO, :] = jnp.zeros((conv_ch // LANES, CONV_HALO, LANES), F32)
        qkbuf[:, 0:QK_HALO, :] = jnp.zeros((2 * mw // LANES, QK_HALO, LANES), F32)
        cext[...] = jnp.zeros(cext.shape, F32)
        mstate[...] = jnp.zeros(mstate.shape, F32)

    h = h_ref[...]
    hn = _rms(h, pre_g_ref[...]).astype(BF16)
    z = _dot(hn, w_main_ref[...])
    zg = _dot(hn, w_gate_ref[...])

    u = z[:, 0:conv_ch] * _sigmoid(z[:, conv_ch:2 * conv_ch])
    acc = _causal_dwconv(ubuf, u, conv_w_ref, conv_b_ref, conv_k, CONV_HALO)
    mu = jnp.mean(acc, axis=-1, keepdims=True)
    cen = acc - mu
    var = jnp.mean(cen * cen, axis=-1, keepdims=True)
    yn = cen * lax.rsqrt(var + LN_EPS) * ln_g_ref[...] + ln_b_ref[...]
    mixin[:, 0:conv_ch] = _silu(yn).astype(BF16)

    qk = _silu(_causal_dwconv(qkbuf, z[:, 2 * conv_ch:2 * conv_ch + 2 * mw], qk_w_ref, qk_b_ref, qk_k, QK_HALO))
    q = qk[:, 0:mw].astype(BF16)
    k = qk[:, mw:2 * mw] * (hd ** -0.5)
    v = z[:, 2 * conv_ch + 2 * mw:2 * conv_ch + 3 * mw].astype(BF16)
    og = _sigmoid(z[:, 2 * conv_ch + 3 * mw:2 * conv_ch + 4 * mw])

    zgb = zg + gbias_ref[...]
    log_f = jnp.minimum(zgb, 0.0) - jnp.log1p(jnp.exp(-jnp.abs(zgb)))
    lane = lax.broadcasted_iota(jnp.int32, zgb.shape, 1)
    gates_t = jnp.where(lane >= FGATE_LANE, log_f, zgb).T
    li_all = gates_t[0:8, :]
    lf_all = gates_t[FGATE_LANE:FGATE_LANE + 8, :]

    row_i = lax.broadcasted_iota(jnp.int32, (MCHUNK, MCHUNK), 0)
    col_i = lax.broadcasted_iota(jnp.int32, (MCHUNK, MCHUNK), 1)
    causal = col_i <= row_i
    lane8 = lax.broadcasted_iota(jnp.int32, (8, MCHUNK), 1)
    ones_col = (lax.broadcasted_iota(jnp.int32, (MCHUNK, hd), 1) == 0).astype(BF16)

    for c in range(tm // MCHUNK):
        rows = slice(c * MCHUNK, (c + 1) * MCHUNK)
        li = li_all[:, rows]
        lf = lf_all[:, rows]
        b = lf
        for d in (1, 2, 4, 8, 16, 32, 64):
            b = b + jnp.where(lane8 >= d, pltpu.roll(b, d, axis=1), 0.0)
        b_end = jnp.sum(lf, axis=1, keepdims=True)
        m_prev = mstate[:, 0:1]
        g = b_end - b + li
        m_new = jnp.maximum(b_end + m_prev, jnp.max(g, axis=1, keepdims=True))
        decay = jnp.exp(b_end + m_prev - m_new)
        wg = jnp.exp(g - m_new)
        mstate[...] = jnp.broadcast_to(m_new, mstate.shape)

        for hh in range(M_HEADS):
            cols = slice(hh * hd, (hh + 1) * hd)
            qh = q[rows, cols]
            kt = k[rows, cols].T
            vh = v[rows, cols]
            b_col = jnp.sum(jnp.where(causal, lf[hh:hh + 1, :], 0.0), axis=1, keepdims=True)
            dmat = jnp.where(causal, b_col - b[hh:hh + 1, :] + li[hh:hh + 1, :], -jnp.inf)
            inter = b_col + m_prev[hh:hh + 1, :]
            m_t = jnp.maximum(inter, jnp.max(dmat, axis=1, keepdims=True))
            s = _dot(qh, kt.astype(BF16))
            w = jnp.exp(dmat - m_t) * s
            a = jnp.exp(inter - m_t)
            ce = cext[hh]
            qc = _dot(qh, ce.astype(BF16))
            num = _dot(w.astype(BF16), vh) + a * qc[:, 0:hd]
            den = jnp.sum(w, axis=1, keepdims=True) + a * qc[:, hd:hd + 1]
            hv = num * (1.0 / jnp.maximum(jnp.abs(den), jnp.exp(-m_t)))
            hv = hv * lax.rsqrt(jnp.mean(hv * hv, axis=-1, keepdims=True) + RMS_EPS)
            hv = hv * mnorm_g_ref[:, cols]
            mixin[rows, conv_ch + hh * hd:conv_ch + (hh + 1) * hd] = (og[rows, cols] * hv).astype(BF16)

            ktw = (kt * wg[hh:hh + 1, :]).astype(BF16)
            vext = jnp.concatenate([vh, ones_col], axis=1)
            cext[hh] = decay[hh:hh + 1, :] * ce + _dot(ktw, vext)

    m = _dot(mixin[...], w_out_ref[...])
    o_ref[...] = h + _rms(m, post_g_ref[...])


def _mixer(h3d, pre_g, w_main, w_gate, conv_w, conv_b, ln_g, ln_b, qk_w, qk_b, gbias, mnorm_g, w_out, post_g,
           *, tm, conv_k, qk_k):
    bsz, seq, d = h3d.shape
    conv_ch = conv_w.shape[1]
    mw = M_HEADS * M_HEAD_DIM
    assert tm % MCHUNK == 0 and seq % tm == 0 and tm >= CONV_HALO
    assert conv_k - 1 <= CONV_HALO and qk_k - 1 <= QK_HALO
    tile = pl.BlockSpec((None, tm, d), lambda b, j: (b, j, 0))
    consts = [pre_g, w_main, w_gate, conv_w, conv_b, ln_g, ln_b, qk_w, qk_b, gbias, mnorm_g, w_out, post_g]
    kern = functools.partial(_mixer_kernel, tm=tm, conv_ch=conv_ch, conv_k=conv_k, qk_k=qk_k)
    return pl.pallas_call(
        kern,
        out_shape=jax.ShapeDtypeStruct((bsz, seq, d), F32),
        grid=(bsz, seq // tm),
        in_specs=[tile] + [_resident(c.shape) for c in consts],
        out_specs=tile,
        scratch_shapes=[
            pltpu.VMEM((conv_ch // LANES, CONV_HALO + tm, LANES), F32),
            pltpu.VMEM((2 * mw // LANES, QK_HALO + tm, LANES), F32),
            pltpu.VMEM((M_HEADS, M_HEAD_DIM, 2 * M_HEAD_DIM), F32),
            pltpu.VMEM((8, MCHUNK), F32),
            pltpu.VMEM((tm, conv_ch + mw), BF16),
        ],
        compiler_params=pltpu.CompilerParams(
            dimension_semantics=("arbitrary", "arbitrary"), vmem_limit_bytes=VMEM_LIMIT_V7X),
        name="mixer",
    )(h3d, *consts)


def _memkv_kernel(mem_ref, g_ref, wk_ref, wv_ref, kt_ref, v_ref, *, scale):
    memn = _rms(mem_ref[...], g_ref[...]).astype(BF16)
    k = _dot(memn, wk_ref[...]) * scale
    kt_ref[...] = k.T.astype(BF16)
    v_ref[...] = _dot(memn, wv_ref[...]).astype(BF16)


def _memkv(mem, g, wk, wv, *, scale):
    bsz, n_mem, d = mem.shape
    return pl.pallas_call(
        functools.partial(_memkv_kernel, scale=scale),
        out_shape=(jax.ShapeDtypeStruct((bsz, d, n_mem), BF16), jax.ShapeDtypeStruct((bsz, n_mem, d), BF16)),
        grid=(bsz,),
        in_specs=[pl.BlockSpec((None, n_mem, d), lambda b: (b, 0, 0)), _resident((1, d)),
                  _resident((d, d)), _resident((d, d))],
        out_specs=(pl.BlockSpec((None, d, n_mem), lambda b: (b, 0, 0)),
                   pl.BlockSpec((None, n_mem, d), lambda b: (b, 0, 0))),
        compiler_params=pltpu.CompilerParams(
            dimension_semantics=("arbitrary",), vmem_limit_bytes=VMEM_LIMIT_V7X),
        name="memkv",
    )(mem, g, wk, wv)


def _xattn_kernel(h_ref, pre_g_ref, wq_ref, kt_ref, v_ref, wo_ref, post_g_ref, o_ref):
    h = h_ref[...]
    d = h.shape[-1]
    xd = d // X_HEADS
    hn = _rms(h, pre_g_ref[...]).astype(BF16)
    q = _dot(hn, wq_ref[...]).astype(BF16)
    outs = []
    for hh in range(X_HEADS):
        cols = slice(hh * xd, (hh + 1) * xd)
        s = _dot(q[:, cols], kt_ref[cols, :])
        p = jnp.exp(s - jnp.max(s, axis=-1, keepdims=True))
        p = p * (1.0 / jnp.sum(p, axis=-1, keepdims=True))
        outs.append(_dot(p.astype(BF16), v_ref[:, cols]))
    att = jnp.concatenate(outs, axis=-1).astype(BF16)
    c = _dot(att, wo_ref[...])
    o_ref[...] = h + _rms(c, post_g_ref[...])


def _xattn(h3d, pre_g, wq, kt, v, wo, post_g, *, tm):
    bsz, seq, d = h3d.shape
    n_mem = v.shape[1]
    assert seq % tm == 0
    tile = pl.BlockSpec((None, tm, d), lambda b, j: (b, j, 0))
    return pl.pallas_call(
        _xattn_kernel,
        out_shape=jax.ShapeDtypeStruct((bsz, seq, d), F32),
        grid=(bsz, seq // tm),
        in_specs=[tile, _resident((1, d)), _resident((d, d)),
                  pl.BlockSpec((None, d, n_mem), lambda b, j: (b, 0, 0)),
                  pl.BlockSpec((None, n_mem, d), lambda b, j: (b, 0, 0)),
                  _resident((d, d)), _resident((1, d))],
        out_specs=tile,
        compiler_params=pltpu.CompilerParams(
            dimension_semantics=("arbitrary", "arbitrary"), vmem_limit_bytes=VMEM_LIMIT_V7X),
        name="xattn",
    )(h3d, pre_g, wq, kt, v, wo, post_g)


def _row(x):
    return x.reshape(1, -1).astype(F32)


def _pad_rows(x, n):
    return jnp.pad(x.astype(F32), ((0, n - x.shape[0]), (0, 0)))


def kernel(x, mem, ffn1_pre_g, ffn1_w_gate, ffn1_w_up, ffn1_w_down, ffn1_post_g, mix_pre_g, w_in, conv_w, conv_b, conv_ln_g, conv_ln_b, qk_conv_w, qk_conv_b, b_igate, b_fgate, mlstm_norm_g, w_out, mix_post_g, xattn_pre_g, mem_norm_g, xattn_wq, xattn_wk, xattn_wv, xattn_wo, xattn_post_g, ffn2_pre_g, ffn2_w_gate, ffn2_w_up, ffn2_w_down, ffn2_post_g):
    bsz, seq, d = x.shape
    depth = w_in.shape[0]
    conv_k, conv_ch = conv_w.shape[1], conv_w.shape[2]
    qk_k = qk_conv_w.shape[1]
    mw = M_HEADS * M_HEAD_DIM
    main_cols = 2 * conv_ch + 4 * mw
    assert w_in.shape[2] == main_cols + 2 * M_HEADS
    tm = TOKEN_TILE

    h = x
    for l in range(depth):
        h = _ffn(h.reshape(bsz * seq, d), _row(ffn1_pre_g[l]), ffn1_w_gate[l].astype(BF16),
                 ffn1_w_up[l].astype(BF16), ffn1_w_down[l].astype(BF16), _row(ffn1_post_g[l]),
                 tm=tm).reshape(bsz, seq, d)

        w_gate = jnp.zeros((d, LANES), F32)
        w_gate = w_gate.at[:, 0:M_HEADS].set(w_in[l][:, main_cols:main_cols + M_HEADS])
        w_gate = w_gate.at[:, FGATE_LANE:FGATE_LANE + M_HEADS].set(w_in[l][:, main_cols + M_HEADS:])
        gbias = jnp.zeros((1, LANES), F32)
        gbias = gbias.at[0, 0:M_HEADS].set(b_igate[l]).at[0, FGATE_LANE:FGATE_LANE + M_HEADS].set(b_fgate[l])
        h = _mixer(h, _row(mix_pre_g[l]), w_in[l][:, :main_cols].astype(BF16), w_gate.astype(BF16),
                   _pad_rows(conv_w[l], 32), _row(conv_b[l]), _row(conv_ln_g[l]), _row(conv_ln_b[l]),
                   _pad_rows(qk_conv_w[l], 8), _row(qk_conv_b[l]), gbias, _row(mlstm_norm_g[l]),
                   w_out[l].astype(BF16), _row(mix_post_g[l]), tm=tm, conv_k=conv_k, qk_k=qk_k)

        kt, v = _memkv(mem, _row(mem_norm_g[l]), xattn_wk[l].astype(BF16), xattn_wv[l].astype(BF16),
                       scale=(d // X_HEADS) ** -0.5)
        h = _xattn(h, _row(xattn_pre_g[l]), xattn_wq[l].astype(BF16), kt, v, xattn_wo[l].astype(BF16),
                   _row(xattn_post_g[l]), tm=tm)

        h = _ffn(h.reshape(bsz * seq, d), _row(ffn2_pre_g[l]), ffn2_w_gate[l].astype(BF16),
                 ffn2_w_up[l].astype(BF16), ffn2_w_down[l].astype(BF16), _row(ffn2_post_g[l]),
                 tm=tm).reshape(bsz, seq, d)
    return h
```

```python
import functools

import jax
import jax.numpy as jnp
from jax import lax
from jax.experimental import pallas as pl
from jax.experimental.pallas import tpu as pltpu

F32 = jnp.float32
BF16 = jnp.bfloat16

RMS_EPS = 1e-6
LN_EPS = 1e-5

LANES = 128
MXU_COLS = 256
M_HEADS = 4
M_HEAD_DIM = 128
X_HEADS = 4
CONV_HALO = 32
QK_HALO = 8
MCHUNK = 128
FGATE_LANE = 8
TOKEN_TILE = 1024

VMEM_LIMIT_V7X = 60 * 1024 * 1024


def _rms(x, g):
    ms = jnp.mean(x * x, axis=-1, keepdims=True)
    return x * lax.rsqrt(ms + RMS_EPS) * g


def _sigmoid(x):
    return 0.5 * jnp.tanh(0.5 * x) + 0.5


def _silu(x):
    return x * _sigmoid(x)


def _dot(a, b):
    return jnp.dot(a, b, preferred_element_type=F32)


def _resident(shape):
    return pl.BlockSpec(shape, lambda *_: (0,) * len(shape), pipeline_mode=pl.Buffered(1))


def _ffn_kernel(h_ref, pre_g_ref, wg_ref, wu_ref, wd_ref, post_g_ref, o_ref):
    h = h_ref[...]
    hn = _rms(h, pre_g_ref[...]).astype(BF16)
    dff = wg_ref.shape[1]
    split = pl.cdiv(dff // MXU_COLS, 2) * MXU_COLS
    f = None
    for lo, hi in ((0, split), (split, dff)):
        half_gate = (0.5 * _dot(hn, wg_ref[:, lo:hi])).astype(BF16)
        up = _dot(hn, wu_ref[:, lo:hi]).astype(BF16)
        a = (half_gate + half_gate * jnp.tanh(half_gate)) * up
        part = _dot(a, wd_ref[lo:hi, :])
        f = part if f is None else f + part
    o_ref[...] = h + 0.5 * _rms(f, post_g_ref[...])


def _ffn(h2d, pre_g, w_gate, w_up, w_down, post_g, *, tm):
    t, d = h2d.shape
    dff = w_gate.shape[1]
    assert t % tm == 0 and dff % MXU_COLS == 0
    tile = pl.BlockSpec((tm, d), lambda i: (i, 0))
    return pl.pallas_call(
        _ffn_kernel,
        out_shape=jax.ShapeDtypeStruct((t, d), F32),
        grid=(t // tm,),
        in_specs=[tile, _resident((1, d)), _resident((d, dff)), _resident((d, dff)),
                  _resident((dff, d)), _resident((1, d))],
        out_specs=tile,
        compiler_params=pltpu.CompilerParams(
            dimension_semantics=("arbitrary",), vmem_limit_bytes=VMEM_LIMIT_V7X),
        name="ffn",
    )(h2d, pre_g, w_gate, w_up, w_down, post_g)


def _causal_dwconv(buf, x, w_ref, b_ref, k, halo):
    tm, ch = x.shape
    base = halo - (k - 1)
    cols = []
    for cb in range(ch // LANES):
        lanes = slice(cb * LANES, (cb + 1) * LANES)
        buf[cb, halo:halo + tm, :] = x[:, lanes]
        acc = jnp.broadcast_to(b_ref[:, lanes], (tm, LANES))
        for j in range(k):
            acc = acc + buf[cb, pl.ds(base + j, tm), :] * w_ref[j:j + 1, lanes]
        buf[cb, 0:halo, :] = buf[cb, tm:tm + halo, :]
        cols.append(acc)
    return jnp.concatenate(cols, axis=1)


def _mixer_kernel(h_ref, pre_g_ref, w_main_ref, w_gate_ref, conv_w_ref, conv_b_ref, ln_g_ref, ln_b_ref,
                  qk_w_ref, qk_b_ref, gbias_ref, mnorm_g_ref, w_out_ref, post_g_ref, o_ref,
                  ubuf, qkbuf, cext, mstate, mixin, *, tm, conv_ch, conv_k, qk_k):
    mw = M_HEADS * M_HEAD_DIM
    hd = M_HEAD_DIM

    @pl.when(pl.program_id(1) == 0)
    def _():
        ubuf[:, 0:CONV_HALO, :] = jnp.zeros((conv_ch // LANES, CONV_HALO, LANES), F32)
        qkbuf[:, 0:QK_HALO, :] = jnp.zeros((2 * mw // LANES, QK_HALO, LANES), F32)
        cext[...] = jnp.zeros(cext.shape, F32)
        mstate[...] = jnp.zeros(mstate.shape, F32)

    h = h_ref[...]
    hn = _rms(h, pre_g_ref[...]).astype(BF16)
    z = _dot(hn, w_main_ref[...])
    zg = _dot(hn, w_gate_ref[...])

    u = z[:, 0:conv_ch] * _sigmoid(z[:, conv_ch:2 * conv_ch])
    acc = _causal_dwconv(ubuf, u, conv_w_ref, conv_b_ref, conv_k, CONV_HALO)
    mu = jnp.mean(acc, axis=-1, keepdims=True)
    cen = acc - mu
    var = jnp.mean(cen * cen, axis=-1, keepdims=True)
    yn = cen * lax.rsqrt(var + LN_EPS) * ln_g_ref[...] + ln_b_ref[...]
    mixin[:, 0:conv_ch] = _silu(yn).astype(BF16)

    qk = _silu(_causal_dwconv(qkbuf, z[:, 2 * conv_ch:2 * conv_ch + 2 * mw], qk_w_ref, qk_b_ref, qk_k, QK_HALO))
    q = qk[:, 0:mw].astype(BF16)
    k = qk[:, mw:2 * mw] * (hd ** -0.5)
    v = z[:, 2 * conv_ch + 2 * mw:2 * conv_ch + 3 * mw].astype(BF16)
    og = _sigmoid(z[:, 2 * conv_ch + 3 * mw:2 * conv_ch + 4 * mw])

    zgb = zg + gbias_ref[...]
    log_f = jnp.minimum(zgb, 0.0) - jnp.log1p(jnp.exp(-jnp.abs(zgb)))
    lane = lax.broadcasted_iota(jnp.int32, zgb.shape, 1)
    gates_t = jnp.where(lane >= FGATE_LANE, log_f, zgb).T
    li_all = gates_t[0:8, :]
    lf_all = gates_t[FGATE_LANE:FGATE_LANE + 8, :]

    row_i = lax.broadcasted_iota(jnp.int32, (MCHUNK, MCHUNK), 0)
    col_i = lax.broadcasted_iota(jnp.int32, (MCHUNK, MCHUNK), 1)
    causal = col_i <= row_i
    lane8 = lax.broadcasted_iota(jnp.int32, (8, MCHUNK), 1)
    ones_col = (lax.broadcasted_iota(jnp.int32, (MCHUNK, hd), 1) == 0).astype(BF16)

    for c in range(tm // MCHUNK):
        rows = slice(c * MCHUNK, (c + 1) * MCHUNK)
        li = li_all[:, rows]
        lf = lf_all[:, rows]
        b = lf
        for d in (1, 2, 4, 8, 16, 32, 64):
            b = b + jnp.where(lane8 >= d, pltpu.roll(b, d, axis=1), 0.0)
        b_end = jnp.sum(lf, axis=1, keepdims=True)
        m_prev = mstate[:, 0:1]
        g = b_end - b + li
        m_new = jnp.maximum(b_end + m_prev, jnp.max(g, axis=1, keepdims=True))
        decay = jnp.exp(b_end + m_prev - m_new)
        wg = jnp.exp(g - m_new)
        mstate[...] = jnp.broadcast_to(m_new, mstate.shape)

        for hh in range(M_HEADS):
            cols = slice(hh * hd, (hh + 1) * hd)
            qh = q[rows, cols]
            kt = k[rows, cols].T
            vh = v[rows, cols]
            b_col = jnp.sum(jnp.where(causal, lf[hh:hh + 1, :], 0.0), axis=1, keepdims=True)
            dmat = jnp.where(causal, b_col - b[hh:hh + 1, :] + li[hh:hh + 1, :], -jnp.inf)
            inter = b_col + m_prev[hh:hh + 1, :]
            m_t = jnp.maximum(inter, jnp.max(dmat, axis=1, keepdims=True))
            s = _dot(qh, kt.astype(BF16))
            w = jnp.exp(dmat - m_t) * s
            a = jnp.exp(inter - m_t)
            ce = cext[hh]
            qc = _dot(qh, ce.astype(BF16))
            num = _dot(w.astype(BF16), vh) + a * qc[:, 0:hd]
            den = jnp.sum(w, axis=1, keepdims=True) + a * qc[:, hd:hd + 1]
            hv = num * (1.0 / jnp.maximum(jnp.abs(den), jnp.exp(-m_t)))
            hv = hv * lax.rsqrt(jnp.mean(hv * hv, axis=-1, keepdims=True) + RMS_EPS)
            hv = hv * mnorm_g_ref[:, cols]
            mixin[rows, conv_ch + hh * hd:conv_ch + (hh + 1) * hd] = (og[rows, cols] * hv).astype(BF16)

            ktw = (kt * wg[hh:hh + 1, :]).astype(BF16)
            vext = jnp.concatenate([vh, ones_col], axis=1)
            cext[hh] = decay[hh:hh + 1, :] * ce + _dot(ktw, vext)

    m = _dot(mixin[...], w_out_ref[...])
    o_ref[...] = h + _rms(m, post_g_ref[...])


def _mixer(h3d, pre_g, w_main, w_gate, conv_w, conv_b, ln_g, ln_b, qk_w, qk_b, gbias, mnorm_g, w_out, post_g,
           *, tm, conv_k, qk_k):
    bsz, seq, d = h3d.shape
    conv_ch = conv_w.shape[1]
    mw = M_HEADS * M_HEAD_DIM
    assert tm % MCHUNK == 0 and seq % tm == 0 and tm >= CONV_HALO
    assert conv_k - 1 <= CONV_HALO and qk_k - 1 <= QK_HALO
    tile = pl.BlockSpec((None, tm, d), lambda b, j: (b, j, 0))
    consts = [pre_g, w_main, w_gate, conv_w, conv_b, ln_g, ln_b, qk_w, qk_b, gbias, mnorm_g, w_out, post_g]
    kern = functools.partial(_mixer_kernel, tm=tm, conv_ch=conv_ch, conv_k=conv_k, qk_k=qk_k)
    return pl.pallas_call(
        kern,
        out_shape=jax.ShapeDtypeStruct((bsz, seq, d), F32),
        grid=(bsz, seq // tm),
        in_specs=[tile] + [_resident(c.shape) for c in consts],
        out_specs=tile,
        scratch_shapes=[
            pltpu.VMEM((conv_ch // LANES, CONV_HALO + tm, LANES), F32),
            pltpu.VMEM((2 * mw // LANES, QK_HALO + tm, LANES), F32),
            pltpu.VMEM((M_HEADS, M_HEAD_DIM, 2 * M_HEAD_DIM), F32),
            pltpu.VMEM((8, MCHUNK), F32),
            pltpu.VMEM((tm, conv_ch + mw), BF16),
        ],
        compiler_params=pltpu.CompilerParams(
            dimension_semantics=("arbitrary", "arbitrary"), vmem_limit_bytes=VMEM_LIMIT_V7X),
        name="mixer",
    )(h3d, *consts)


def _memkv_kernel(mem_ref, g_ref, wk_ref, wv_ref, kt_ref, v_ref, *, scale):
    memn = _rms(mem_ref[...], g_ref[...]).astype(BF16)
    k = _dot(memn, wk_ref[...]) * scale
    kt_ref[...] = k.T.astype(BF16)
    v_ref[...] = _dot(memn, wv_ref[...]).astype(BF16)


def _memkv(mem, g, wk, wv, *, scale):
    bsz, n_mem, d = mem.shape
    return pl.pallas_call(
        functools.partial(_memkv_kernel, scale=scale),
        out_shape=(jax.ShapeDtypeStruct((bsz, d, n_mem), BF16), jax.ShapeDtypeStruct((bsz, n_mem, d), BF16)),
        grid=(bsz,),
        in_specs=[pl.BlockSpec((None, n_mem, d), lambda b: (b, 0, 0)), _resident((1, d)),
                  _resident((d, d)), _resident((d, d))],
        out_specs=(pl.BlockSpec((None, d, n_mem), lambda b: (b, 0, 0)),
                   pl.BlockSpec((None, n_mem, d), lambda b: (b, 0, 0))),
        compiler_params=pltpu.CompilerParams(
            dimension_semantics=("arbitrary",), vmem_limit_bytes=VMEM_LIMIT_V7X),
        name="memkv",
    )(mem, g, wk, wv)


def _xattn_kernel(h_ref, pre_g_ref, wq_ref, kt_ref, v_ref, wo_ref, post_g_ref, o_ref):
    h = h_ref[...]
    d = h.shape[-1]
    xd = d // X_HEADS
    hn = _rms(h, pre_g_ref[...]).astype(BF16)
    q = _dot(hn, wq_ref[...]).astype(BF16)
    outs = []
    for hh in range(X_HEADS):
        cols = slice(hh * xd, (hh + 1) * xd)
        s = _dot(q[:, cols], kt_ref[cols, :])
        p = jnp.exp(s - jnp.max(s, axis=-1, keepdims=True))
        p = p * (1.0 / jnp.sum(p, axis=-1, keepdims=True))
        outs.append(_dot(p.astype(BF16), v_ref[:, cols]))
    att = jnp.concatenate(outs, axis=-1).astype(BF16)
    c = _dot(att, wo_ref[...])
    o_ref[...] = h + _rms(c, post_g_ref[...])


def _xattn(h3d, pre_g, wq, kt, v, wo, post_g, *, tm):
    bsz, seq, d = h3d.shape
    n_mem = v.shape[1]
    assert seq % tm == 0
    tile = pl.BlockSpec((None, tm, d), lambda b, j: (b, j, 0))
    return pl.pallas_call(
        _xattn_kernel,
        out_shape=jax.ShapeDtypeStruct((bsz, seq, d), F32),
        grid=(bsz, seq // tm),
        in_specs=[tile, _resident((1, d)), _resident((d, d)),
                  pl.BlockSpec((None, d, n_mem), lambda b, j: (b, 0, 0)),
                  pl.BlockSpec((None, n_mem, d), lambda b, j: (b, 0, 0)),
                  _resident((d, d)), _resident((1, d))],
        out_specs=tile,
        compiler_params=pltpu.CompilerParams(
            dimension_semantics=("arbitrary", "arbitrary"), vmem_limit_bytes=VMEM_LIMIT_V7X),
        name="xattn",
    )(h3d, pre_g, wq, kt, v, wo, post_g)


def _row(x):
    return x.reshape(1, -1).astype(F32)


def _pad_rows(x, n):
    return jnp.pad(x.astype(F32), ((0, n - x.shape[0]), (0, 0)))


def kernel(x, mem, ffn1_pre_g, ffn1_w_gate, ffn1_w_up, ffn1_w_down, ffn1_post_g, mix_pre_g, w_in, conv_w, conv_b, conv_ln_g, conv_ln_b, qk_conv_w, qk_conv_b, b_igate, b_fgate, mlstm_norm_g, w_out, mix_post_g, xattn_pre_g, mem_norm_g, xattn_wq, xattn_wk, xattn_wv, xattn_wo, xattn_post_g, ffn2_pre_g, ffn2_w_gate, ffn2_w_up, ffn2_w_down, ffn2_post_g):
    bsz, seq, d = x.shape
    depth = w_in.shape[0]
    conv_k, conv_ch = conv_w.shape[1], conv_w.shape[2]
    qk_k = qk_conv_w.shape[1]
    mw = M_HEADS * M_HEAD_DIM
    main_cols = 2 * conv_ch + 4 * mw
    assert w_in.shape[2] == main_cols + 2 * M_HEADS
    tm = TOKEN_TILE

    h = x
    for l in range(depth):
        h = _ffn(h.reshape(bsz * seq, d), _row(ffn1_pre_g[l]), ffn1_w_gate[l].astype(BF16),
                 ffn1_w_up[l].astype(BF16), ffn1_w_down[l].astype(BF16), _row(ffn1_post_g[l]),
                 tm=tm).reshape(bsz, seq, d)

        w_gate = jnp.zeros((d, LANES), F32)
        w_gate = w_gate.at[:, 0:M_HEADS].set(w_in[l][:, main_cols:main_cols + M_HEADS])
        w_gate = w_gate.at[:, FGATE_LANE:FGATE_LANE + M_HEADS].set(w_in[l][:, main_cols + M_HEADS:])
        gbias = jnp.zeros((1, LANES), F32)
        gbias = gbias.at[0, 0:M_HEADS].set(b_igate[l]).at[0, FGATE_LANE:FGATE_LANE + M_HEADS].set(b_fgate[l])
        h = _mixer(h, _row(mix_pre_g[l]), w_in[l][:, :main_cols].astype(BF16), w_gate.astype(BF16),
                   _pad_rows(conv_w[l], 32), _row(conv_b[l]), _row(conv_ln_g[l]), _row(conv_ln_b[l]),
                   _pad_rows(qk_conv_w[l], 8), _row(qk_conv_b[l]), gbias, _row(mlstm_norm_g[l]),
                   w_out[l].astype(BF16), _row(mix_post_g[l]), tm=tm, conv_k=conv_k, qk_k=qk_k)

        kt, v = _memkv(mem, _row(mem_norm_g[l]), xattn_wk[l].astype(BF16), xattn_wv[l].astype(BF16),
                       scale=(d // X_HEADS) ** -0.5)
        h = _xattn(h, _row(xattn_pre_g[l]), xattn_wq[l].astype(BF16), kt, v, xattn_wo[l].astype(BF16),
                   _row(xattn_post_g[l]), tm=tm)

        h = _ffn(h.reshape(bsz * seq, d), _row(ffn2_pre_g[l]), ffn2_w_gate[l].astype(BF16),
                 ffn2_w_up[l].astype(BF16), ffn2_w_down[l].astype(BF16), _row(ffn2_post_g[l]),
                 tm=tm).reshape(bsz, seq, d)
    return h
```
